```python
import jax, jax.numpy as jnp
from jax import lax
import numpy as np

D_MODEL = 4096
BATCH = 4
SEQ = 2048
DEPTH = 4
DEC_BATCH = 2
DEC_SEQ = 8192
PAST_LEN = 128

D_FOURIER = D_MODEL // 2
N_FOURIER_GROUPS = 4
D_FOURIER_GROUP = D_FOURIER // N_FOURIER_GROUPS
D_CONV = D_MODEL // 2
CONV_WIDTH = 31
CONV_PAD = CONV_WIDTH // 2
D_FF = ((8 * D_MODEL // 3 + 255) // 256) * 256
D_IN = D_FOURIER + 2 * D_CONV + 2 * D_MODEL
ALPHA = (2 * DEPTH) ** 0.25
BETA = (8 * DEPTH) ** -0.25
LN_EPS = 1e-5

kernel_name = "fnet_conformer_gated_encoder"


def layer_norm(x, g, b):
    xf = x.astype(jnp.float32)
    mu = jnp.mean(xf, axis=-1, keepdims=True)
    var = jnp.mean(jnp.square(xf - mu), axis=-1, keepdims=True)
    y = (xf - mu) * lax.rsqrt(var + LN_EPS) * g.astype(jnp.float32) + b.astype(jnp.float32)
    return y.astype(x.dtype)


def swiglu_ffn(x, w_in, w_out):
    gate, up = jnp.split(x @ w_in, 2, axis=-1)
    return (jax.nn.silu(gate) * up) @ w_out


def fourier_mix(u):
    b, s, _ = u.shape
    ug = u.astype(jnp.float32).reshape(b, s, N_FOURIER_GROUPS, D_FOURIER_GROUP)
    f = jnp.fft.fft2(ug, axes=(1, 3), norm="ortho")
    return jnp.real(f).reshape(b, s, D_FOURIER).astype(u.dtype)


def conv_module(v, w_dw, b_dw, ln_g, ln_b):
    val, gate = jnp.split(v, 2, axis=-1)
    h = val * jax.nn.sigmoid(gate)
    h = lax.conv_general_dilated(
        h, w_dw[:, None, :].astype(h.dtype),
        window_strides=(1,), padding=[(CONV_PAD, CONV_PAD)],
        dimension_numbers=("NWC", "WIO", "NWC"),
        feature_group_count=D_CONV) + b_dw
    return jax.nn.silu(layer_norm(h, ln_g, ln_b))


def encoder_layer(x, ln_ffn1_g, ln_ffn1_b, w_ffn1_in, w_ffn1_out, w_in, b_in, w_fourier,
                  w_dw, b_dw, ln_conv_g, ln_conv_b, w_conv_out, w_out, ln_mix_g, ln_mix_b,
                  w_ffn2_in, w_ffn2_out, ln_ffn2_g, ln_ffn2_b):
    x = layer_norm(ALPHA * x + 0.5 * swiglu_ffn(x, w_ffn1_in, w_ffn1_out), ln_ffn1_g, ln_ffn1_b)
    proj = x @ w_in + b_in
    u = proj[..., :D_FOURIER]
    v = proj[..., D_FOURIER:D_FOURIER + 2 * D_CONV]
    g_a, g_c = jnp.split(jax.nn.sigmoid(proj[..., D_FOURIER + 2 * D_CONV:]), 2, axis=-1)
    a = fourier_mix(u) @ w_fourier
    c = conv_module(v, w_dw, b_dw, ln_conv_g, ln_conv_b) @ w_conv_out
    mix = (g_a * a + g_c * c) @ w_out
    x = layer_norm(ALPHA * x + mix, ln_mix_g, ln_mix_b)
    x = layer_norm(ALPHA * x + 0.5 * swiglu_ffn(x, w_ffn2_in, w_ffn2_out), ln_ffn2_g, ln_ffn2_b)
    return x


def trunk(x, ln_ffn1_g, ln_ffn1_b, w_ffn1_in, w_ffn1_out, w_in, b_in, w_fourier,
          w_dw, b_dw, ln_conv_g, ln_conv_b, w_conv_out, w_out, ln_mix_g, ln_mix_b,
          w_ffn2_in, w_ffn2_out, ln_ffn2_g, ln_ffn2_b):
    for l in range(DEPTH):
        x = encoder_layer(x, ln_ffn1_g[l], ln_ffn1_b[l], w_ffn1_in[l], w_ffn1_out[l], w_in[l], b_in[l],
                          w_fourier[l], w_dw[l], b_dw[l], ln_conv_g[l], ln_conv_b[l], w_conv_out[l],
                          w_out[l], ln_mix_g[l], ln_mix_b[l], w_ffn2_in[l], w_ffn2_out[l],
                          ln_ffn2_g[l], ln_ffn2_b[l])
    return x


def setup_inputs(seed: int = 0) -> dict:
    key = jax.random.key(seed)
    ks = jax.random.split(key, 21)
    f32 = jnp.float32

    def nrm(k, shape, scale):
        return jax.random.normal(k, shape, f32) * scale

    def gain(k, n):
        return 1.0 + nrm(k, (DEPTH, n), 0.02)

    return {
        "x_prompt": jax.random.normal(ks[0], (BATCH, SEQ, D_MODEL), f32),
        "x_sample": jax.random.normal(ks[1], (DEC_BATCH, DEC_SEQ, D_MODEL), f32),
        "ln_ffn1_g": gain(ks[2], D_MODEL),
        "ln_ffn1_b": nrm(ks[3], (DEPTH, D_MODEL), 0.02),
        "w_ffn1_in": nrm(ks[4], (DEPTH, D_MODEL, 2 * D_FF), D_MODEL ** -0.5),
        "w_ffn1_out": nrm(ks[5], (DEPTH, D_FF, D_MODEL), BETA * D_FF ** -0.5),
        "w_in": nrm(ks[6], (DEPTH, D_MODEL, D_IN), D_MODEL ** -0.5),
        "b_in": nrm(ks[7], (DEPTH, D_IN), 0.02),
        "w_fourier": nrm(ks[8], (DEPTH, D_FOURIER, D_MODEL), BETA * D_FOURIER ** -0.5),
        "w_dw": nrm(ks[9], (DEPTH, CONV_WIDTH, D_CONV), CONV_WIDTH ** -0.5),
        "b_dw": nrm(ks[10], (DEPTH, D_CONV), 0.02),
        "ln_conv_g": gain(ks[11], D_CONV),
        "ln_conv_b": nrm(ks[12], (DEPTH, D_CONV), 0.02),
        "w_conv_out": nrm(ks[13], (DEPTH, D_CONV, D_MODEL), BETA * D_CONV ** -0.5),
        "w_out": nrm(ks[14], (DEPTH, D_MODEL, D_MODEL), BETA * D_MODEL ** -0.5),
        "ln_mix_g": gain(ks[15], D_MODEL),
        "ln_mix_b": nrm(ks[16], (DEPTH, D_MODEL), 0.02),
        "w_ffn2_in": nrm(ks[17], (DEPTH, D_MODEL, 2 * D_FF), D_MODEL ** -0.5),
        "w_ffn2_out": nrm(ks[18], (DEPTH, D_FF, D_MODEL), BETA * D_FF ** -0.5),
        "ln_ffn2_g": gain(ks[19], D_MODEL),
        "ln_ffn2_b": nrm(ks[20], (DEPTH, D_MODEL), 0.02),
    }


def reference(x_prompt, x_sample, ln_ffn1_g, ln_ffn1_b, w_ffn1_in, w_ffn1_out, w_in, b_in, w_fourier,
              w_dw, b_dw, ln_conv_g, ln_conv_b, w_conv_out, w_out, ln_mix_g, ln_mix_b,
              w_ffn2_in, w_ffn2_out, ln_ffn2_g, ln_ffn2_b):
    y_prompt = trunk(x_prompt, ln_ffn1_g, ln_ffn1_b, w_ffn1_in, w_ffn1_out, w_in, b_in, w_fourier,
                     w_dw, b_dw, ln_conv_g, ln_conv_b, w_conv_out, w_out, ln_mix_g, ln_mix_b,
                     w_ffn2_in, w_ffn2_out, ln_ffn2_g, ln_ffn2_b)
    y_sample = trunk(x_sample, ln_ffn1_g, ln_ffn1_b, w_ffn1_in, w_ffn1_out, w_in, b_in, w_fourier,
                     w_dw, b_dw, ln_conv_g, ln_conv_b, w_conv_out, w_out, ln_mix_g, ln_mix_b,
                     w_ffn2_in, w_ffn2_out, ln_ffn2_g, ln_ffn2_b)
    return (y_prompt, y_sample)
```

```python
import functools
import math

import jax
import jax.numpy as jnp
from jax import lax
from jax.experimental import pallas as pl
from jax.experimental.pallas import tpu as pltpu

LN_EPS = 1e-5
N_FOURIER_GROUPS = 4
ROW_CHUNK = 16
CONV_HALO_ROWS = 16
V7X_VMEM_LIMIT_BYTES = 56 * 1024 * 1024
BF16 = jnp.bfloat16
F32 = jnp.float32


def _params(n_axes):
    return pltpu.CompilerParams(dimension_semantics=("arbitrary",) * n_axes,
                                vmem_limit_bytes=V7X_VMEM_LIMIT_BYTES)


def _tile(n, pref, align):
    if n <= pref:
        return n
    t = (pref // align) * align
    while t > align and n % t:
        t -= align
    assert n % t == 0, (n, pref, align)
    return t


def _dot(a, b):
    return jnp.dot(a, b, preferred_element_type=F32)


def _layer_norm(y, g, b):
    mu = jnp.mean(y, axis=-1, keepdims=True)
    d = y - mu
    var = jnp.mean(d * d, axis=-1, keepdims=True)
    return d * lax.rsqrt(var + LN_EPS) * g + b


def _for_row_chunks(n_rows, chunk, fn):
    def body(c, carry):
        fn(pl.ds(pl.multiple_of(c * chunk, chunk), chunk))
        return carry

    lax.fori_loop(0, n_rows // chunk, body, 0)


def _ffn_body(x_ref, wg_ref, wu_ref, wo_ref, g_ref, b_ref, o_ref, xb_ref, *, alpha, nj):
    j = pl.program_id(1)
    tm = x_ref.shape[0]

    @pl.when(j == 0)
    def _():
        def init(rows):
            xb_ref[rows, :] = x_ref[rows, :].astype(BF16)
            o_ref[rows, :] = jnp.zeros((ROW_CHUNK, o_ref.shape[1]), F32)

        _for_row_chunks(tm, ROW_CHUNK, init)

    xb = xb_ref[...]
    gate = _dot(xb, wg_ref[...])
    up = _dot(xb, wu_ref[...])
    h = (gate * jax.nn.sigmoid(gate) * up).astype(BF16)
    o_ref[...] += _dot(h, wo_ref[...])

    @pl.when(j == nj - 1)
    def _():
        def finish(rows):
            y = alpha * x_ref[rows, :] + 0.5 * o_ref[rows, :]
            o_ref[rows, :] = _layer_norm(y, g_ref[...], b_ref[...])

        _for_row_chunks(tm, ROW_CHUNK, finish)


def _ffn(x, w_in, w_out, ln_g, ln_b, *, alpha):
    m, d = x.shape
    f = w_out.shape[0]
    tm = _tile(m, 512, 8)
    tf = _tile(f, 256, 128)
    nj = f // tf
    return pl.pallas_call(
        functools.partial(_ffn_body, alpha=alpha, nj=nj),
        grid=(m // tm, nj),
        in_specs=[
            pl.BlockSpec((tm, d), lambda i, j: (i, 0)),
            pl.BlockSpec((d, tf), lambda i, j: (0, j)),
            pl.BlockSpec((d, tf), lambda i, j: (0, j + nj)),
            pl.BlockSpec((tf, d), lambda i, j: (j, 0)),
            pl.BlockSpec((1, d), lambda i, j: (0, 0)),
            pl.BlockSpec((1, d), lambda i, j: (0, 0)),
        ],
        out_specs=pl.BlockSpec((tm, d), lambda i, j: (i, 0)),
        out_shape=jax.ShapeDtypeStruct((m, d), F32),
        scratch_shapes=[pltpu.VMEM((tm, d), BF16)],
        compiler_params=_params(2),
        name="ffn",
    )(x, w_in, w_in, w_out, ln_g, ln_b)


def _proj_body(x_ref, w_ref, b_ref, u_ref, h_ref, gt_ref, xb_ref, val_ref, *, ng):
    j = pl.program_id(1)

    @pl.when(j == 0)
    def _():
        def cast(rows):
            xb_ref[rows, :] = x_ref[rows, :].astype(BF16)

        _for_row_chunks(x_ref.shape[0], ROW_CHUNK, cast)

    y = _dot(xb_ref[...], w_ref[...]) + b_ref[...]

    @pl.when(j < ng)
    def _():
        u_ref[...] = y.astype(u_ref.dtype)

    @pl.when((j >= ng) & (j < 2 * ng))
    def _():
        val_ref[j - ng] = y

    @pl.when((j >= 2 * ng) & (j < 3 * ng))
    def _():
        h_ref[...] = val_ref[j - 2 * ng] * jax.nn.sigmoid(y)

    @pl.when(j >= 3 * ng)
    def _():
        gt_ref[...] = jax.nn.sigmoid(y).astype(gt_ref.dtype)


def _proj(x, w_in, b_in):
    m, d = x.shape
    ng = N_FOURIER_GROUPS
    tn = d // (2 * ng)
    assert w_in.shape[1] == 7 * ng * tn
    tm = _tile(m, 512, 8)
    return pl.pallas_call(
        functools.partial(_proj_body, ng=ng),
        grid=(m // tm, 7 * ng),
        in_specs=[
            pl.BlockSpec((tm, d), lambda i, j: (i, 0)),
            pl.BlockSpec((d, tn), lambda i, j: (0, j)),
            pl.BlockSpec((1, tn), lambda i, j: (0, j)),
        ],
        out_specs=[
            pl.BlockSpec((tm, tn), lambda i, j: (i, jnp.minimum(j, ng - 1))),
            pl.BlockSpec((tm, tn), lambda i, j: (i, jnp.clip(j - 2 * ng, 0, ng - 1))),
            pl.BlockSpec((tm, tn), lambda i, j: (i, jnp.clip(j - 3 * ng, 0, 4 * ng - 1))),
        ],
        out_shape=[
            jax.ShapeDtypeStruct((m, ng * tn), BF16),
            jax.ShapeDtypeStruct((m, ng * tn), F32),
            jax.ShapeDtypeStruct((m, 4 * ng * tn), F32),
        ],
        scratch_shapes=[pltpu.VMEM((tm, d), BF16), pltpu.VMEM((ng, tm, tn), F32)],
        compiler_params=_params(2),
        name="proj",
    )(x, w_in, b_in)


def _dft_tables(s, cg):
    def cos_sin(n):
        idx = jnp.arange(n, dtype=jnp.int32)
        r = (idx[:, None] * idx[None, :]) % n
        ang = r.astype(F32) * (2.0 * math.pi / n)
        scale = 1.0 / math.sqrt(n)
        return jnp.cos(ang) * scale, jnp.sin(ang) * scale

    cc, sc = cos_sin(cg)
    cs, ss = cos_sin(s)
    chan = jnp.concatenate([cc, sc], axis=1).astype(BF16)
    seq = jnp.concatenate([cs, -ss], axis=1).astype(BF16)
    return chan, seq


def _cdft_body(u_ref, t_ref, z_ref, *, ng, cg):
    for g in range(ng):
        r = _dot(u_ref[:, g * cg:(g + 1) * cg], t_ref[...])
        z_ref[0, 0, :, g * cg:(g + 1) * cg] = r[:, :cg].astype(z_ref.dtype)
        z_ref[0, 1, :, g * cg:(g + 1) * cg] = r[:, cg:].astype(z_ref.dtype)


def _cdft(u, chan_table, batch, s):
    m, c = u.shape
    ng = N_FOURIER_GROUPS
    cg = c // ng
    tm = _tile(s, 1024, 8)
    ns = s // tm
    return pl.pallas_call(
        functools.partial(_cdft_body, ng=ng, cg=cg),
        grid=(batch, ns),
        in_specs=[
            pl.BlockSpec((tm, c), lambda b, i: (b * ns + i, 0)),
            pl.BlockSpec((cg, 2 * cg), lambda b, i: (0, 0)),
        ],
        out_specs=pl.BlockSpec((1, 2, tm, c), lambda b, i: (b, 0, i, 0)),
        out_shape=jax.ShapeDtypeStruct((batch, 2, s, c), BF16),
        compiler_params=_params(2),
        name="cdft",
    )(u, chan_table)


def _sdft_body(t_ref, z_ref, y_ref, acc_ref, *, nk):
    k = pl.program_id(2)

    @pl.when(k == 0)
    def _():
        acc_ref[...] = jnp.zeros_like(acc_ref)

    acc_ref[...] += _dot(t_ref[...], z_ref[0])

    @pl.when(k == nk - 1)
    def _():
        y_ref[...] = acc_ref[...].astype(y_ref.dtype)


def _sdft(z, seq_table):
    batch, _, s, c = z.shape
    z = z.reshape(batch, 2 * s, c)
    tm = _tile(s, 1024, 8)
    tk = _tile(2 * s, 1024, 128)
    ns, nk = s // tm, 2 * s // tk
    return pl.pallas_call(
        functools.partial(_sdft_body, nk=nk),
        grid=(batch, ns, nk),
        in_specs=[
            pl.BlockSpec((tm, tk), lambda b, i, k: (i, k)),
            pl.BlockSpec((1, tk, c), lambda b, i, k: (b, k, 0)),
        ],
        out_specs=pl.BlockSpec((tm, c), lambda b, i, k: (b * ns + i, 0)),
        out_shape=jax.ShapeDtypeStruct((batch * s, c), BF16),
        scratch_shapes=[pltpu.VMEM((tm, c), F32)],
        compiler_params=_params(3),
        name="sdft",
    )(seq_table, z)


def _conv_body(hp_ref, hm_ref, hn_ref, w_ref, bdw_ref, g_ref, b_ref, o_ref, buf_ref, y_ref,
               *, ts, ns, width, lc, rc):
    i = pl.program_id(1)
    halo = CONV_HALO_ROWS
    c = hm_ref.shape[1]
    buf_ref[0:halo, :] = jnp.where(i > 0, hp_ref[...], 0.0)
    buf_ref[halo:halo + ts, :] = hm_ref[...]
    buf_ref[halo + ts:halo + ts + halo, :] = jnp.where(i < ns - 1, hn_ref[...], 0.0)
    off = halo - width // 2
    nq = rc // 8
    for c0 in range(0, c, lc):
        for r0 in range(0, ts, rc):
            acc = [jnp.zeros((8, lc), F32) for _ in range(nq)]
            for t in range(width):
                wv = jnp.broadcast_to(w_ref[t:t + 1, c0:c0 + lc], (8, lc))
                for q in range(nq):
                    s0 = r0 + 8 * q + off + t
                    acc[q] = acc[q] + wv * buf_ref[s0:s0 + 8, c0:c0 + lc]
            for q in range(nq):
                y_ref[r0 + 8 * q:r0 + 8 * q + 8, c0:c0 + lc] = acc[q] + bdw_ref[:, c0:c0 + lc]

    def finish(rows):
        z = _layer_norm(y_ref[rows, :], g_ref[...], b_ref[...])
        o_ref[rows, :] = (z * jax.nn.sigmoid(z)).astype(o_ref.dtype)

    _for_row_chunks(ts, ROW_CHUNK, finish)


def _conv(h, w_dw, b_dw, ln_g, ln_b, batch, s):
    m, c = h.shape
    width = w_dw.shape[0]
    halo = CONV_HALO_ROWS
    assert width // 2 <= halo and s % halo == 0
    ts = _tile(s, 128, halo)
    ns = s // ts
    rc = _tile(ts, 64, 8)
    lc = _tile(c, 512, 128)
    hb, sb = ts // halo, s // halo
    last = m // halo - 1
    return pl.pallas_call(
        functools.partial(_conv_body, ts=ts, ns=ns, width=width, lc=lc, rc=rc),
        grid=(batch, ns),
        in_specs=[
            pl.BlockSpec((halo, c), lambda b, i: (jnp.maximum(b * sb + i * hb - 1, 0), 0)),
            pl.BlockSpec((ts, c), lambda b, i: (b * ns + i, 0)),
            pl.BlockSpec((halo, c), lambda b, i: (jnp.minimum(b * sb + (i + 1) * hb, last), 0)),
            pl.BlockSpec((width, c), lambda b, i: (0, 0)),
            pl.BlockSpec((1, c), lambda b, i: (0, 0)),
            pl.BlockSpec((1, c), lambda b, i: (0, 0)),
            pl.BlockSpec((1, c), lambda b, i: (0, 0)),
        ],
        out_specs=pl.BlockSpec((ts, c), lambda b, i: (b * ns + i, 0)),
        out_shape=jax.ShapeDtypeStruct((m, c), BF16),
        scratch_shapes=[pltpu.VMEM((ts + 2 * halo, c), F32), pltpu.VMEM((ts, c), F32)],
        compiler_params=_params(2),
        name="conv",
    )(h, h, h, w_dw, b_dw, ln_g, ln_b)


def _mix_body(y_ref, c_ref, wf_ref, wc_ref, ga_ref, gc_ref, m_ref):
    a = _dot(y_ref[...], wf_ref[...])
    c = _dot(c_ref[...], wc_ref[...])
    m_ref[...] = (ga_ref[...] * a + gc_ref[...] * c).astype(m_ref.dtype)


def _mix(y, cin, w_fourier, w_conv_out, gates):
    m, kf = y.shape
    kc = cin.shape[1]
    d = w_fourier.shape[1]
    tm = _tile(m, 1024, 8)
    tn = _tile(d, 512, 128)
    nj = d // tn
    return pl.pallas_call(
        _mix_body,
        grid=(m // tm, nj),
        in_specs=[
            pl.BlockSpec((tm, kf), lambda i, j: (i, 0)),
            pl.BlockSpec((tm, kc), lambda i, j: (i, 0)),
            pl.BlockSpec((kf, tn), lambda i, j: (0, j)),
            pl.BlockSpec((kc, tn), lambda i, j: (0, j)),
            pl.BlockSpec((tm, tn), lambda i, j: (i, j)),
            pl.BlockSpec((tm, tn), lambda i, j: (i, j + nj)),
        ],
        out_specs=pl.BlockSpec((tm, tn), lambda i, j: (i, j)),
        out_shape=jax.ShapeDtypeStruct((m, d), BF16),
        compiler_params=_params(2),
        name="mix",
    )(y, cin, w_fourier, w_conv_out, gates, gates)


def _mmln_body(a_ref, w_ref, r_ref, g_ref, b_ref, o_ref, *, alpha, nk):
    k = pl.program_id(1)

    @pl.when(k == 0)
    def _():
        o_ref[...] = jnp.zeros_like(o_ref)

    o_ref[...] += _dot(a_ref[...], w_ref[...])

    @pl.when(k == nk - 1)
    def _():
        def finish(rows):
            y = alpha * r_ref[rows, :] + o_ref[rows, :]
            o_ref[rows, :] = _layer_norm(y, g_ref[...], b_ref[...])

        _for_row_chunks(o_ref.shape[0], ROW_CHUNK, finish)


def _mmln(a, w, resid, ln_g, ln_b, *, alpha):
    m, kk = a.shape
    d = w.shape[1]
    tm = _tile(m, 512, 8)
    tk = _tile(kk, 512, 128)
    nk = kk // tk
    return pl.pallas_call(
        functools.partial(_mmln_body, alpha=alpha, nk=nk),
        grid=(m // tm, nk),
        in_specs=[
            pl.BlockSpec((tm, tk), lambda i, k: (i, k)),
            pl.BlockSpec((tk, d), lambda i, k: (k, 0)),
            pl.BlockSpec((tm, d), lambda i, k: (i, 0)),
            pl.BlockSpec((1, d), lambda i, k: (0, 0)),
            pl.BlockSpec((1, d), lambda i, k: (0, 0)),
        ],
        out_specs=pl.BlockSpec((tm, d), lambda i, k: (i, 0)),
        out_shape=jax.ShapeDtypeStruct((m, d), F32),
        compiler_params=_params(2),
        name="mmln",
    )(a, w, resid, ln_g, ln_b)


def _encoder_layer(x, p, tables, batch, s, alpha):
    chan_table, seq_table = tables
    x = _ffn(x, p["w_ffn1_in"], p["w_ffn1_out"], p["ln_ffn1_g"], p["ln_ffn1_b"], alpha=alpha)
    u, h, gates = _proj(x, p["w_in"], p["b_in"])
    y = _sdft(_cdft(u, chan_table, batch, s), seq_table)
    cin = _conv(h, p["w_dw"], p["b_dw"], p["ln_conv_g"], p["ln_conv_b"], batch, s)
    mixed = _mix(y, cin, p["w_fourier"], p["w_conv_out"], gates)
    x = _mmln(mixed, p["w_out"], x, p["ln_mix_g"], p["ln_mix_b"], alpha=alpha)
    x = _ffn(x, p["w_ffn2_in"], p["w_ffn2_out"], p["ln_ffn2_g"], p["ln_ffn2_b"], alpha=alpha)
    return x


_MATMUL_WEIGHTS = ("w_ffn1_in", "w_ffn1_out", "w_in", "w_fourier", "w_conv_out", "w_out",
                   "w_ffn2_in", "w_ffn2_out")
_ROW_PARAMS = ("ln_ffn1_g", "ln_ffn1_b", "b_in", "b_dw", "ln_conv_g", "ln_conv_b",
               "ln_mix_g", "ln_mix_b", "ln_ffn2_g", "ln_ffn2_b")


def kernel(x_prompt, x_sample, ln_ffn1_g, ln_ffn1_b, w_ffn1_in, w_ffn1_out, w_in, b_in, w_fourier,
           w_dw, b_dw, ln_conv_g, ln_conv_b, w_conv_out, w_out, ln_mix_g, ln_mix_b,
           w_ffn2_in, w_ffn2_out, ln_ffn2_g, ln_ffn2_b):
    stacked = dict(ln_ffn1_g=ln_ffn1_g, ln_ffn1_b=ln_ffn1_b, w_ffn1_in=w_ffn1_in, w_ffn1_out=w_ffn1_out,
                   w_in=w_in, b_in=b_in, w_fourier=w_fourier, w_dw=w_dw, b_dw=b_dw,
                   ln_conv_g=ln_conv_g, ln_conv_b=ln_conv_b, w_conv_out=w_conv_out, w_out=w_out,
                   ln_mix_g=ln_mix_g, ln_mix_b=ln_mix_b, w_ffn2_in=w_ffn2_in, w_ffn2_out=w_ffn2_out,
                   ln_ffn2_g=ln_ffn2_g, ln_ffn2_b=ln_ffn2_b)
    depth = w_ffn1_in.shape[0]
    alpha = float((2 * depth) ** 0.25)
    layers = []
    for l in range(depth):
        p = {k: stacked[k][l].astype(BF16) for k in _MATMUL_WEIGHTS}
        p.update({k: stacked[k][l].reshape(1, -1) for k in _ROW_PARAMS})
        p["w_dw"] = w_dw[l]
        layers.append(p)

    cg = w_fourier.shape[1] // N_FOURIER_GROUPS

    def trunk(x):
        batch, s, d = x.shape
        tables = _dft_tables(s, cg)
        x = x.reshape(batch * s, d)
        for p in layers:
            x = _encoder_layer(x, p, tables, batch, s, alpha)
        return x.reshape(batch, s, d)

    return trunk(x_prompt), trunk(x_sample)
```

```python
import functools
import math

import jax
import jax.numpy as jnp
from jax import lax
from jax.experimental import pallas as pl
from jax.experimental.pallas import tpu as pltpu

LN_EPS = 1e-5
N_FOURIER_GROUPS = 4
SUBLANES = 8
CONV_HALO_ROWS = 16
LN_STEP_ROWS = 128
V7X_VMEM_LIMIT_BYTES = 56 * 1024 * 1024
BF16 = jnp.bfloat16
F32 = jnp.float32


def _params(n_axes):
    return pltpu.CompilerParams(dimension_semantics=("arbitrary",) * n_axes,
                                vmem_limit_bytes=V7X_VMEM_LIMIT_BYTES)


def _tile(n, pref, align):
    if n <= pref:
        return n
    t = (pref // align) * align
    while t > align and n % t:
        t -= align
    assert n % t == 0, (n, pref, align)
    return t


def _dot(a, b):
    return jnp.dot(a, b, preferred_element_type=F32)


def _layer_norm(y, g, b):
    mu = jnp.mean(y, axis=-1, keepdims=True)
    d = y - mu
    var = jnp.mean(d * d, axis=-1, keepdims=True)
    return d * lax.rsqrt(var + LN_EPS) * g + b


def _for_row_chunks(n_rows, chunk, fn, unroll=1):
    def body(c, carry):
        fn(pl.multiple_of(c * chunk, chunk))
        return carry

    lax.fori_loop(0, n_rows // chunk, body, 0, unroll=unroll)


def _residual_ln_step(e, acc_ref, r_ref, g_ref, b_ref, o_ref, ob_ref, *, alpha, scale):
    step_rows = r_ref.shape[0]
    base = pl.multiple_of(e * step_rows, step_rows)

    def rows_fn(r0):
        rows = pl.ds(r0, SUBLANES)
        y = alpha * r_ref[rows, :] + scale * acc_ref[pl.ds(base + r0, SUBLANES), :]
        z = _layer_norm(y, g_ref[...], b_ref[...])
        o_ref[rows, :] = z
        ob_ref[rows, :] = z.astype(BF16)

    _for_row_chunks(step_rows, SUBLANES, rows_fn, unroll=4)


def _ln_step_specs(tm, d, nj):
    per_block = tm // LN_STEP_ROWS
    idx = lambda i, j: (i * per_block + jnp.maximum(j - nj, 0), 0)
    return pl.BlockSpec((LN_STEP_ROWS, d), idx)


def _ffn_body(xb_ref, wg_ref, wu_ref, wo_ref, r_ref, g_ref, b_ref, o_ref, ob_ref, acc_ref, *, alpha, nj):
    j = pl.program_id(1)

    @pl.when(j == 0)
    def _():
        acc_ref[...] = jnp.zeros_like(acc_ref)

    @pl.when(j < nj)
    def _():
        xb = xb_ref[...]
        gate = _dot(xb, wg_ref[...])
        up = _dot(xb, wu_ref[...])
        h = (gate * jax.nn.sigmoid(gate) * up).astype(BF16)
        acc_ref[...] += _dot(h, wo_ref[...])

    @pl.when(j >= nj)
    def _():
        _residual_ln_step(j - nj, acc_ref, r_ref, g_ref, b_ref, o_ref, ob_ref, alpha=alpha, scale=0.5)


def _ffn(x, xb, w_in, w_out, ln_g, ln_b, *, alpha):
    m, d = x.shape
    f = w_out.shape[0]
    tm = _tile(m, 1024, LN_STEP_ROWS)
    tf = _tile(f, 256, 128)
    nj = f // tf
    ne = tm // LN_STEP_ROWS
    jw = lambda j: jnp.minimum(j, nj - 1)
    ln_spec = _ln_step_specs(tm, d, nj)
    return pl.pallas_call(
        functools.partial(_ffn_body, alpha=alpha, nj=nj),
        grid=(m // tm, nj + ne),
        in_specs=[
            pl.BlockSpec((tm, d), lambda i, j: (i, 0), pipeline_mode=pl.Buffered(1)),
            pl.BlockSpec((d, tf), lambda i, j: (0, jw(j))),
            pl.BlockSpec((d, tf), lambda i, j: (0, jw(j) + nj)),
            pl.BlockSpec((tf, d), lambda i, j: (jw(j), 0)),
            ln_spec,
            pl.BlockSpec((1, d), lambda i, j: (0, 0)),
            pl.BlockSpec((1, d), lambda i, j: (0, 0)),
        ],
        out_specs=[ln_spec, ln_spec],
        out_shape=[jax.ShapeDtypeStruct((m, d), F32), jax.ShapeDtypeStruct((m, d), BF16)],
        scratch_shapes=[pltpu.VMEM((tm, d), F32)],
        compiler_params=_params(2),
        name="ffn",
    )(xb, w_in, w_in, w_out, x, ln_g, ln_b)


def _proj_body(xb_ref, w_ref, b_ref, wv_ref, bv_ref, wg_ref, bg_ref, u_ref, h_ref, gt_ref, *, ng):
    j = pl.program_id(1)

    @pl.when(j < ng)
    def _():
        u_ref[...] = (_dot(xb_ref[...], w_ref[...]) + b_ref[...]).astype(u_ref.dtype)

    @pl.when((j >= ng) & (j < 2 * ng))
    def _():
        xb = xb_ref[...]
        val = _dot(xb, wv_ref[...]) + bv_ref[...]
        gate = _dot(xb, wg_ref[...]) + bg_ref[...]
        h_ref[...] = val * jax.nn.sigmoid(gate)

    @pl.when(j >= 2 * ng)
    def _():
        gt_ref[...] = jax.nn.sigmoid(_dot(xb_ref[...], w_ref[...]) + b_ref[...]).astype(gt_ref.dtype)


def _proj(xb, w_in, b_in):
    m, d = xb.shape
    ng = N_FOURIER_GROUPS
    tn = d // (2 * ng)
    assert w_in.shape[1] == 7 * ng * tn
    tm = _tile(m, 1024, 8)
    main = lambda i, j: (0, jnp.where(j < 2 * ng, jnp.minimum(j, ng - 1), j + ng))
    val = lambda i, j: (0, ng + jnp.clip(j - ng, 0, ng - 1))
    gate = lambda i, j: (0, 2 * ng + jnp.clip(j - ng, 0, ng - 1))
    return pl.pallas_call(
        functools.partial(_proj_body, ng=ng),
        grid=(m // tm, 6 * ng),
        in_specs=[
            pl.BlockSpec((tm, d), lambda i, j: (i, 0), pipeline_mode=pl.Buffered(1)),
            pl.BlockSpec((d, tn), main),
            pl.BlockSpec((1, tn), main),
            pl.BlockSpec((d, tn), val),
            pl.BlockSpec((1, tn), val),
            pl.BlockSpec((d, tn), gate),
            pl.BlockSpec((1, tn), gate),
        ],
        out_specs=[
            pl.BlockSpec((tm, tn), lambda i, j: (i, jnp.minimum(j, ng - 1))),
            pl.BlockSpec((tm, tn), lambda i, j: (i, jnp.clip(j - ng, 0, ng - 1))),
            pl.BlockSpec((tm, tn), lambda i, j: (i, jnp.clip(j - 2 * ng, 0, 4 * ng - 1))),
        ],
        out_shape=[
            jax.ShapeDtypeStruct((m, ng * tn), BF16),
            jax.ShapeDtypeStruct((m, ng * tn), F32),
            jax.ShapeDtypeStruct((m, 4 * ng * tn), F32),
        ],
        compiler_params=_params(2),
        name="proj",
    )(xb, w_in, b_in, w_in, b_in, w_in, b_in)


def _dft_tables(s, cg):
    def cos_sin(n):
        idx = jnp.arange(n, dtype=jnp.int32)
        r = (idx[:, None] * idx[None, :]) % n
        ang = r.astype(F32) * (2.0 * math.pi / n)
        scale = 1.0 / math.sqrt(n)
        return jnp.cos(ang) * scale, jnp.sin(ang) * scale

    cc, sc = cos_sin(cg)
    cs, ss = cos_sin(s)
    chan = jnp.concatenate([cc, sc], axis=1).astype(BF16)
    seq = jnp.concatenate([cs, -ss], axis=1).astype(BF16)
    return chan, seq


def _cdft_body(u_ref, t_ref, z_ref, *, ng, cg):
    for g in range(ng):
        r = _dot(u_ref[:, g * cg:(g + 1) * cg], t_ref[...])
        z_ref[0, 0, :, g * cg:(g + 1) * cg] = r[:, :cg].astype(z_ref.dtype)
        z_ref[0, 1, :, g * cg:(g + 1) * cg] = r[:, cg:].astype(z_ref.dtype)


def _cdft(u, chan_table, batch, s):
    m, c = u.shape
    ng = N_FOURIER_GROUPS
    cg = c // ng
    tm = _tile(s, 1024, 8)
    ns = s // tm
    return pl.pallas_call(
        functools.partial(_cdft_body, ng=ng, cg=cg),
        grid=(batch, ns),
        in_specs=[
            pl.BlockSpec((tm, c), lambda b, i: (b * ns + i, 0)),
            pl.BlockSpec((cg, 2 * cg), lambda b, i: (0, 0)),
        ],
        out_specs=pl.BlockSpec((1, 2, tm, c), lambda b, i: (b, 0, i, 0)),
        out_shape=jax.ShapeDtypeStruct((batch, 2, s, c), BF16),
        compiler_params=_params(2),
        name="cdft",
    )(u, chan_table)


def _sdft_body(t_ref, z_ref, y_ref, acc_ref, *, nk):
    k = pl.program_id(2)

    @pl.when(k == 0)
    def _():
        acc_ref[...] = jnp.zeros_like(acc_ref)

    acc_ref[...] += _dot(t_ref[...], z_ref[0])

    @pl.when(k == nk - 1)
    def _():
        y_ref[...] = acc_ref[...].astype(y_ref.dtype)


def _sdft(z, seq_table):
    batch, _, s, c = z.shape
    z = z.reshape(batch, 2 * s, c)
    tm = _tile(s, 1024, 8)
    tk = _tile(2 * s, 1024, 128)
    ns, nk = s // tm, 2 * s // tk
    return pl.pallas_call(
        functools.partial(_sdft_body, nk=nk),
        grid=(batch, ns, nk),
        in_specs=[
            pl.BlockSpec((tm, tk), lambda b, i, k: (i, k)),
            pl.BlockSpec((1, tk, c), lambda b, i, k: (b, k, 0)),
        ],
        out_specs=pl.BlockSpec((tm, c), lambda b, i, k: (b * ns + i, 0)),
        out_shape=jax.ShapeDtypeStruct((batch * s, c), BF16),
        scratch_shapes=[pltpu.VMEM((tm, c), F32)],
        compiler_params=_params(3),
        name="sdft",
    )(seq_table, z)


def _conv_body(hp_ref, hm_ref, hn_ref, w8_ref, bdw_ref, g_ref, b_ref, o_ref, sh_ref, y_ref,
               *, ts, ns, width, lc, rc):
    i = pl.program_id(1)
    halo = CONV_HALO_ROWS
    c = hm_ref.shape[1]
    span = ts + 2 * halo - SUBLANES
    sh_ref[0, 0:halo, :] = jnp.where(i > 0, hp_ref[...], 0.0)
    sh_ref[0, halo:halo + ts, :] = hm_ref[...]
    sh_ref[0, halo + ts:halo + ts + halo, :] = jnp.where(i < ns - 1, hn_ref[...], 0.0)
    piece = 7 * SUBLANES
    for s in range(1, SUBLANES):
        for c0 in range(0, c, lc):
            for r0 in range(0, span, piece):
                n = min(piece, span - r0)
                sh_ref[s, r0:r0 + n, c0:c0 + lc] = sh_ref[0, r0 + s:r0 + s + n, c0:c0 + lc]

    off = halo - width // 2
    nq = rc // SUBLANES
    for c0 in range(0, c, lc):
        lanes = slice(c0, c0 + lc)

        def row_chunk(r0, lanes=lanes):
            acc = [None] * nq
            for t in range(width):
                a, s = divmod(t + off, SUBLANES)
                wv = w8_ref[t, :, lanes]
                for q in range(nq):
                    term = wv * sh_ref[s, pl.ds(r0 + SUBLANES * (q + a), SUBLANES), lanes]
                    acc[q] = term if acc[q] is None else acc[q] + term
            for q in range(nq):
                y_ref[pl.ds(r0 + SUBLANES * q, SUBLANES), lanes] = acc[q] + bdw_ref[:, lanes]

        _for_row_chunks(ts, rc, row_chunk)

    def finish(r0):
        rows = pl.ds(r0, SUBLANES)
        z = _layer_norm(y_ref[rows, :], g_ref[...], b_ref[...])
        o_ref[rows, :] = (z * jax.nn.sigmoid(z)).astype(o_ref.dtype)

    _for_row_chunks(ts, SUBLANES, finish, unroll=8)


def _conv(h, w_dw, b_dw, ln_g, ln_b, batch, s):
    m, c = h.shape
    width = w_dw.shape[0]
    halo = CONV_HALO_ROWS
    assert width // 2 <= halo and s % halo == 0
    ts = _tile(s, 256, 4 * halo)
    ns = s // ts
    rc = 4 * SUBLANES
    lc = _tile(c, 512, 128)
    hb, sb = ts // halo, s // halo
    last = m // halo - 1
    w8 = jnp.broadcast_to(w_dw[:, None, :], (width, SUBLANES, c))
    return pl.pallas_call(
        functools.partial(_conv_body, ts=ts, ns=ns, width=width, lc=lc, rc=rc),
        grid=(batch, ns),
        in_specs=[
            pl.BlockSpec((halo, c), lambda b, i: (jnp.maximum(b * sb + i * hb - 1, 0), 0)),
            pl.BlockSpec((ts, c), lambda b, i: (b * ns + i, 0)),
            pl.BlockSpec((halo, c), lambda b, i: (jnp.minimum(b * sb + (i + 1) * hb, last), 0)),
            pl.BlockSpec((width, SUBLANES, c), lambda b, i: (0, 0, 0)),
            pl.BlockSpec((1, c), lambda b, i: (0, 0)),
            pl.BlockSpec((1, c), lambda b, i: (0, 0)),
            pl.BlockSpec((1, c), lambda b, i: (0, 0)),
        ],
        out_specs=pl.BlockSpec((ts, c), lambda b, i: (b * ns + i, 0)),
        out_shape=jax.ShapeDtypeStruct((m, c), BF16),
        scratch_shapes=[pltpu.VMEM((SUBLANES, ts + 2 * halo, c), F32), pltpu.VMEM((ts, c), F32)],
        compiler_params=_params(2),
        name="conv",
    )(h, h, h, w8, b_dw, ln_g, ln_b)


def _mixout_body(y_ref, c_ref, wf_ref, wc_ref, ga_ref, gc_ref, wo_ref, r_ref, g_ref, b_ref,
                 o_ref, ob_ref, acc_ref, *, alpha, nj):
    j = pl.program_id(1)

    @pl.when(j == 0)
    def _():
        acc_ref[...] = jnp.zeros_like(acc_ref)

    @pl.when(j < nj)
    def _():
        a = _dot(y_ref[...], wf_ref[...])
        c = _dot(c_ref[...], wc_ref[...])
        mixed = (ga_ref[...] * a + gc_ref[...] * c).astype(BF16)
        acc_ref[...] += _dot(mixed, wo_ref[...])

    @pl.when(j >= nj)
    def _():
        _residual_ln_step(j - nj, acc_ref, r_ref, g_ref, b_ref, o_ref, ob_ref, alpha=alpha, scale=1.0)


def _mixout(y, cin, w_fourier, w_conv_out, gates, w_out, x, ln_g, ln_b, *, alpha):
    m, kf = y.shape
    kc = cin.shape[1]
    d = w_out.shape[1]
    tm = _tile(m, 512, LN_STEP_ROWS)
    tn = _tile(d, 512, 128)
    nj = d // tn
    ne = tm // LN_STEP_ROWS
    jw = lambda j: jnp.minimum(j, nj - 1)
    ln_spec = _ln_step_specs(tm, d, nj)
    return pl.pallas_call(
        functools.partial(_mixout_body, alpha=alpha, nj=nj),
        grid=(m // tm, nj + ne),
        in_specs=[
            pl.BlockSpec((tm, kf), lambda i, j: (i, 0)),
            pl.BlockSpec((tm, kc), lambda i, j: (i, 0)),
            pl.BlockSpec((kf, tn), lambda i, j: (0, jw(j))),
            pl.BlockSpec((kc, tn), lambda i, j: (0, jw(j))),
            pl.BlockSpec((tm, tn), lambda i, j: (i, jw(j))),
            pl.BlockSpec((tm, tn), lambda i, j: (i, jw(j) + nj)),
            pl.BlockSpec((tn, d), lambda i, j: (jw(j), 0)),
            ln_spec,
            pl.BlockSpec((1, d), lambda i, j: (0, 0)),
            pl.BlockSpec((1, d), lambda i, j: (0, 0)),
        ],
        out_specs=[ln_spec, ln_spec],
        out_shape=[jax.ShapeDtypeStruct((m, d), F32), jax.ShapeDtypeStruct((m, d), BF16)],
        scratch_shapes=[pltpu.VMEM((tm, d), F32)],
        compiler_params=_params(2),
        name="mixout",
    )(y, cin, w_fourier, w_conv_out, gates, gates, w_out, x, ln_g, ln_b)


def _encoder_layer(x, xb, p, tables, batch, s, alpha):
    chan_table, seq_table = tables
    x, xb = _ffn(x, xb, p["w_ffn1_in"], p["w_ffn1_out"], p["ln_ffn1_g"], p["ln_ffn1_b"], alpha=alpha)
    u, h, gates = _proj(xb, p["w_in"], p["b_in"])
    y = _sdft(_cdft(u, chan_table, batch, s), seq_table)
    cin = _conv(h, p["w_dw"], p["b_dw"], p["ln_conv_g"], p["ln_conv_b"], batch, s)
    x, xb = _mixout(y, cin, p["w_fourier"], p["w_conv_out"], gates, p["w_out"], x,
                    p["ln_mix_g"], p["ln_mix_b"], alpha=alpha)
    x, xb = _ffn(x, xb, p["w_ffn2_in"], p["w_ffn2_out"], p["ln_ffn2_g"], p["ln_ffn2_b"], alpha=alpha)
    return x, xb


_MATMUL_WEIGHTS = ("w_ffn1_in", "w_ffn1_out", "w_in", "w_fourier", "w_conv_out", "w_out",
                   "w_ffn2_in", "w_ffn2_out")
_ROW_PARAMS = ("ln_ffn1_g", "ln_ffn1_b", "b_in", "b_dw", "ln_conv_g", "ln_conv_b",
               "ln_mix_g", "ln_mix_b", "ln_ffn2_g", "ln_ffn2_b")


def kernel(x_prompt, x_sample, ln_ffn1_g, ln_ffn1_b, w_ffn1_in, w_ffn1_out, w_in, b_in, w_fourier,
           w_dw, b_dw, ln_conv_g, ln_conv_b, w_conv_out, w_out, ln_mix_g, ln_mix_b,
           w_ffn2_in, w_ffn2_out, ln_ffn2_g, ln_ffn2_b):
    stacked = dict(ln_ffn1_g=ln_ffn1_g, ln_ffn1_b=ln_ffn1_b, w_ffn1_in=w_ffn1_in, w_ffn1_out=w_ffn1_out,
                   w_in=w_in, b_in=b_in, w_fourier=w_fourier, w_dw=w_dw, b_dw=b_dw,
                   ln_conv_g=ln_conv_g, ln_conv_b=ln_conv_b, w_conv_out=w_conv_out, w_out=w_out,
                   ln_mix_g=ln_mix_g, ln_mix_b=ln_mix_b, w_ffn2_in=w_ffn2_in, w_ffn2_out=w_ffn2_out,
                   ln_ffn2_g=ln_ffn2_g, ln_ffn2_b=ln_ffn2_b)
    depth = w_ffn1_in.shape[0]
    alpha = float((2 * depth) ** 0.25)
    layers = []
    for l in range(depth):
        p = {k: stacked[k][l].astype(BF16) for k in _MATMUL_WEIGHTS}
        p.update({k: stacked[k][l].reshape(1, -1) for k in _ROW_PARAMS})
        p["w_dw"] = w_dw[l]
        layers.append(p)

    cg = w_fourier.shape[1] // N_FOURIER_GROUPS

    def trunk(x):
        batch, s, d = x.shape
        tables = _dft_tables(s, cg)
        x = x.reshape(batch * s, d)
        xb = x.astype(BF16)
        for p in layers:
            x, xb = _encoder_layer(x, xb, p, tables, batch, s, alpha)
        return x.reshape(batch, s, d)

    return trunk(x_prompt), trunk(x_sample)
```

```python
import functools
import math

import jax
import jax.numpy as jnp
from jax import lax
from jax.experimental import pallas as pl
from jax.experimental.pallas import tpu as pltpu

LN_EPS = 1e-5
N_FOURIER_GROUPS = 4
SUBLANES = 8
CONV_HALO_ROWS = 16
MID_ROWS = 16
LN_STEP_ROWS = 128
LN_UNROLL = 8
V7X_VMEM_LIMIT_BYTES = 56 * 1024 * 1024
BF16 = jnp.bfloat16
F32 = jnp.float32


def _params(n_axes):
    return pltpu.CompilerParams(dimension_semantics=("arbitrary",) * n_axes,
                                vmem_limit_bytes=V7X_VMEM_LIMIT_BYTES)


def _tile(n, pref, align):
    if n <= pref:
        return n
    t = (pref // align) * align
    while t > align and n % t:
        t -= align
    assert n % t == 0, (n, pref, align)
    return t


def _dot(a, b):
    return jnp.dot(a, b, preferred_element_type=F32)


def _layer_norm(y, g, b):
    mu = jnp.mean(y, axis=-1, keepdims=True)
    d = y - mu
    var = jnp.mean(d * d, axis=-1, keepdims=True)
    return d * lax.rsqrt(var + LN_EPS) * g + b


def _for_row_chunks(n_rows, chunk, fn, unroll=1):
    def body(c, carry):
        fn(pl.multiple_of(c * chunk, chunk))
        return carry

    lax.fori_loop(0, n_rows // chunk, body, 0, unroll=unroll)


def _residual_ln_step(e, acc_ref, r_ref, g_ref, b_ref, o_ref, ob_ref, *, alpha):
    step_rows = r_ref.shape[0]
    base = pl.multiple_of(e * step_rows, step_rows)

    def rows_fn(r0):
        rows = pl.ds(r0, SUBLANES)
        y = alpha * r_ref[rows, :] + acc_ref[pl.ds(base + r0, SUBLANES), :]
        z = _layer_norm(y, g_ref[...], b_ref[...])
        o_ref[rows, :] = z
        ob_ref[rows, :] = z.astype(BF16)

    _for_row_chunks(step_rows, SUBLANES, rows_fn, unroll=LN_UNROLL)


def _ln_step_specs(tm, d, nj):
    per_block = tm // LN_STEP_ROWS
    idx = lambda i, j: (i * per_block + jnp.maximum(j - nj, 0), 0)
    return pl.BlockSpec((LN_STEP_ROWS, d), idx)


CAST_BLOCK_BYTES = 4 * 1024 * 1024


def _cast_body(w_ref, o_ref, *, scale):
    w = w_ref[...]
    if scale != 1.0:
        w = w * scale
    o_ref[...] = w.astype(BF16)


def _cast_layer(w_stacked, layer, scale=1.0):
    _, r, c = w_stacked.shape
    tr = _tile(r, max(16, CAST_BLOCK_BYTES // (4 * c) // 16 * 16), 16)
    return pl.pallas_call(
        functools.partial(_cast_body, scale=scale),
        grid=(r // tr,),
        in_specs=[pl.BlockSpec((None, tr, c), lambda i: (layer, i, 0))],
        out_specs=pl.BlockSpec((tr, c), lambda i: (i, 0)),
        out_shape=jax.ShapeDtypeStruct((r, c), BF16),
        compiler_params=_params(1),
        name="cast",
    )(w_stacked)


def _ffn_body(xb_ref, wg_ref, wu_ref, wo_ref, r_ref, g_ref, b_ref, o_ref, ob_ref, acc_ref, *, alpha, nj):
    j = pl.program_id(1)

    @pl.when(j == 0)
    def _():
        acc_ref[...] = jnp.zeros_like(acc_ref)

    @pl.when(j < nj)
    def _():
        xb = xb_ref[...]
        gate = _dot(xb, wg_ref[...])
        up = _dot(xb, wu_ref[...])
        h = (gate * jax.nn.sigmoid(gate) * up).astype(BF16)
        acc_ref[...] += _dot(h, wo_ref[...])

    @pl.when(j >= nj)
    def _():
        _residual_ln_step(j - nj, acc_ref, r_ref, g_ref, b_ref, o_ref, ob_ref, alpha=alpha)


def _ffn(x, xb, w_in, w_out, ln_g, ln_b, *, alpha):
    m, d = x.shape
    f = w_out.shape[0]
    tm = _tile(m, 1024, LN_STEP_ROWS)
    tf = _tile(f, 256, 128)
    nj = f // tf
    ne = tm // LN_STEP_ROWS
    jw = lambda j: jnp.minimum(j, nj - 1)
    ln_spec = _ln_step_specs(tm, d, nj)
    return pl.pallas_call(
        functools.partial(_ffn_body, alpha=alpha, nj=nj),
        grid=(m // tm, nj + ne),
        in_specs=[
            pl.BlockSpec((tm, d), lambda i, j: (i, 0), pipeline_mode=pl.Buffered(1)),
            pl.BlockSpec((d, tf), lambda i, j: (0, jw(j))),
            pl.BlockSpec((d, tf), lambda i, j: (0, jw(j) + nj)),
            pl.BlockSpec((tf, d), lambda i, j: (jw(j), 0)),
            ln_spec,
            pl.BlockSpec((1, d), lambda i, j: (0, 0)),
            pl.BlockSpec((1, d), lambda i, j: (0, 0)),
        ],
        out_specs=[ln_spec, ln_spec],
        out_shape=[jax.ShapeDtypeStruct((m, d), F32), jax.ShapeDtypeStruct((m, d), BF16)],
        scratch_shapes=[pltpu.VMEM((tm, d), F32)],
        compiler_params=_params(2),
        name="ffn",
    )(xb, w_in, w_in, w_out, x, ln_g, ln_b)


def _proj_body(xb_ref, w_ref, b_ref, wv_ref, bv_ref, wg_ref, bg_ref, u_ref, h_ref, gt_ref, *, ng):
    j = pl.program_id(1)

    @pl.when(j < ng)
    def _():
        u_ref[...] = (_dot(xb_ref[...], w_ref[...]) + b_ref[...]).astype(u_ref.dtype)

    @pl.when((j >= ng) & (j < 2 * ng))
    def _():
        xb = xb_ref[...]
        val = _dot(xb, wv_ref[...]) + bv_ref[...]
        gate = _dot(xb, wg_ref[...]) + bg_ref[...]
        h_ref[...] = val * jax.nn.sigmoid(gate)

    @pl.when(j >= 2 * ng)
    def _():
        gt_ref[...] = jax.nn.sigmoid(_dot(xb_ref[...], w_ref[...]) + b_ref[...]).astype(gt_ref.dtype)


def _proj(xb, w_in, b_in):
    m, d = xb.shape
    ng = N_FOURIER_GROUPS
    tn = d // (2 * ng)
    assert w_in.shape[1] == 7 * ng * tn
    tm = _tile(m, 1024, 8)
    main = lambda i, j: (0, jnp.where(j < 2 * ng, jnp.minimum(j, ng - 1), j + ng))
    val = lambda i, j: (0, ng + jnp.clip(j - ng, 0, ng - 1))
    gate = lambda i, j: (0, 2 * ng + jnp.clip(j - ng, 0, ng - 1))
    return pl.pallas_call(
        functools.partial(_proj_body, ng=ng),
        grid=(m // tm, 6 * ng),
        in_specs=[
            pl.BlockSpec((tm, d), lambda i, j: (i, 0), pipeline_mode=pl.Buffered(1)),
            pl.BlockSpec((d, tn), main),
            pl.BlockSpec((1, tn), main),
            pl.BlockSpec((d, tn), val),
            pl.BlockSpec((1, tn), val),
            pl.BlockSpec((d, tn), gate),
            pl.BlockSpec((1, tn), gate),
        ],
        out_specs=[
            pl.BlockSpec((tm, tn), lambda i, j: (i, jnp.minimum(j, ng - 1))),
            pl.BlockSpec((tm, tn), lambda i, j: (i, jnp.clip(j - ng, 0, ng - 1))),
            pl.BlockSpec((tm, tn), lambda i, j: (i, jnp.clip(j - 2 * ng, 0, 4 * ng - 1))),
        ],
        out_shape=[
            jax.ShapeDtypeStruct((m, ng * tn), BF16),
            jax.ShapeDtypeStruct((m, ng * tn), F32),
            jax.ShapeDtypeStruct((m, 4 * ng * tn), F32),
        ],
        compiler_params=_params(2),
        name="proj",
    )(xb, w_in, b_in, w_in, b_in, w_in, b_in)


def _dft_tables(s, cg):
    def cos_sin(rows, cols, period):
        r = (jnp.arange(rows, dtype=jnp.int32)[:, None] * jnp.arange(cols, dtype=jnp.int32)[None, :]) % period
        ang = r.astype(F32) * (2.0 * math.pi / period)
        scale = 1.0 / math.sqrt(period)
        return jnp.cos(ang) * scale, jnp.sin(ang) * scale

    cc, sc = cos_sin(cg, cg, cg)
    cs, ss = cos_sin(s, s // 2, s)
    cs = cs * jnp.where(jnp.arange(s // 2) == 0, 0.5, 1.0)[None, :]
    chan = jnp.concatenate([cc, sc], axis=1).astype(BF16)
    seq = jnp.concatenate([cs, -ss], axis=1).astype(BF16)
    return chan, seq


def _cdft_body(u_ref, uf_ref, um_ref, t_ref, z_ref, zm_ref, *, ng, cg):
    t = t_ref[...]
    for g in range(ng):
        cols = slice(g * cg, (g + 1) * cg)
        r = _dot(u_ref[:, cols], t)
        rf = _dot(uf_ref[:, cols], t)
        z_ref[0, 0, :, cols] = (r[:, :cg] + rf[:, :cg]).astype(z_ref.dtype)
        z_ref[0, 1, :, cols] = (r[:, cg:] - rf[:, cg:]).astype(z_ref.dtype)
        zm_ref[0, :, cols] = _dot(um_ref[:, cols], t)[:, :cg]


def _cdft(u, u_mirror, chan_table, batch, s):
    m, c = u.shape
    ng = N_FOURIER_GROUPS
    cg = c // ng
    h = s // 2
    assert h % MID_ROWS == 0
    tm = _tile(h, 1024, 16)
    nh = h // tm
    return pl.pallas_call(
        functools.partial(_cdft_body, ng=ng, cg=cg),
        grid=(batch, nh),
        in_specs=[
            pl.BlockSpec((tm, c), lambda b, i: (b * 2 * nh + i, 0)),
            pl.BlockSpec((tm, c), lambda b, i: (b * nh + i, 0)),
            pl.BlockSpec((MID_ROWS, c), lambda b, i: ((b * s + h) // MID_ROWS, 0)),
            pl.BlockSpec((cg, 2 * cg), lambda b, i: (0, 0)),
        ],
        out_specs=[
            pl.BlockSpec((1, 2, tm, c), lambda b, i: (b, 0, i, 0)),
            pl.BlockSpec((1, MID_ROWS, c), lambda b, i: (b, 0, 0)),
        ],
        out_shape=[
            jax.ShapeDtypeStruct((batch, 2, h, c), BF16),
            jax.ShapeDtypeStruct((batch, MID_ROWS, c), F32),
        ],
        compiler_params=_params(2),
        name="cdft",
    )(u, u_mirror, u, chan_table)


def _sdft_body(t_ref, z_ref, zm_ref, y_ref, acc_ref, *, nk, mid_scale):
    k = pl.program_id(2)

    @pl.when(k == 0)
    def _():
        acc_ref[...] = jnp.zeros_like(acc_ref)

    acc_ref[...] += _dot(t_ref[...], z_ref[0])

    @pl.when(k == nk - 1)
    def _():
        row = lax.broadcasted_iota(jnp.int32, (acc_ref.shape[0], 1), 0)
        sign = (1 - 2 * (row & 1)).astype(F32)
        y_ref[...] = (acc_ref[...] + sign * (zm_ref[0, 0:1, :] * mid_scale)).astype(y_ref.dtype)


def _sdft(z, zm, seq_table):
    batch, _, h, c = z.shape
    s = 2 * h
    z = z.reshape(batch, s, c)
    tm = _tile(s, 1024, 16)
    tk = _tile(s, 2048, 128)
    ns, nk = s // tm, s // tk
    return pl.pallas_call(
        functools.partial(_sdft_body, nk=nk, mid_scale=1.0 / math.sqrt(s)),
        grid=(batch, ns, nk),
        in_specs=[
            pl.BlockSpec((tm, tk), lambda b, i, k: (i, k)),
            pl.BlockSpec((1, tk, c), lambda b, i, k: (b, k, 0)),
            pl.BlockSpec((1, MID_ROWS, c), lambda b, i, k: (b, 0, 0)),
        ],
        out_specs=pl.BlockSpec((tm, c), lambda b, i, k: (b * ns + i, 0)),
        out_shape=jax.ShapeDtypeStruct((batch * s, c), BF16),
        scratch_shapes=[pltpu.VMEM((tm, c), F32)],
        compiler_params=_params(3),
        name="sdft",
    )(seq_table, z, zm)


def _conv_body(hp_ref, hm_ref, hn_ref, w8_ref, bdw_ref, g_ref, b_ref, o_ref, sh_ref, y_ref,
               *, ts, ns, width, lc, rc):
    i = pl.program_id(1)
    halo = CONV_HALO_ROWS
    c = hm_ref.shape[1]
    span = ts + 2 * halo - SUBLANES
    sh_ref[0, 0:halo, :] = jnp.where(i > 0, hp_ref[...], 0.0)
    sh_ref[0, halo:halo + ts, :] = hm_ref[...]
    sh_ref[0, halo + ts:halo + ts + halo, :] = jnp.where(i < ns - 1, hn_ref[...], 0.0)
    piece = 7 * SUBLANES
    for s in range(1, SUBLANES):
        for c0 in range(0, c, lc):
            for r0 in range(0, span, piece):
                n = min(piece, span - r0)
                sh_ref[s, r0:r0 + n, c0:c0 + lc] = sh_ref[0, r0 + s:r0 + s + n, c0:c0 + lc]

    off = halo - width // 2
    nq = rc // SUBLANES
    for c0 in range(0, c, lc):
        lanes = slice(c0, c0 + lc)

        def row_chunk(r0, lanes=lanes):
            acc = [None] * nq
            for t in range(width):
                a, s = divmod(t + off, SUBLANES)
                wv = w8_ref[t, :, lanes]
                for q in range(nq):
                    term = wv * sh_ref[s, pl.ds(r0 + SUBLANES * (q + a), SUBLANES), lanes]
                    acc[q] = term if acc[q] is None else acc[q] + term
            for q in range(nq):
                y_ref[pl.ds(r0 + SUBLANES * q, SUBLANES), lanes] = acc[q] + bdw_ref[:, lanes]

        _for_row_chunks(ts, rc, row_chunk)

    def finish(r0):
        rows = pl.ds(r0, SUBLANES)
        z = _layer_norm(y_ref[rows, :], g_ref[...], b_ref[...])
        o_ref[rows, :] = (z * jax.nn.sigmoid(z)).astype(o_ref.dtype)

    _for_row_chunks(ts, SUBLANES, finish, unroll=8)


def _conv(h, w_dw, b_dw, ln_g, ln_b, batch, s):
    m, c = h.shape
    width = w_dw.shape[0]
    halo = CONV_HALO_ROWS
    assert width // 2 <= halo and s % halo == 0
    ts = _tile(s, 256, 4 * halo)
    ns = s // ts
    rc = 4 * SUBLANES
    lc = _tile(c, 512, 128)
    hb, sb = ts // halo, s // halo
    last = m // halo - 1
    w8 = jnp.broadcast_to(w_dw[:, None, :], (width, SUBLANES, c))
    return pl.pallas_call(
        functools.partial(_conv_body, ts=ts, ns=ns, width=width, lc=lc, rc=rc),
        grid=(batch, ns),
        in_specs=[
            pl.BlockSpec((halo, c), lambda b, i: (jnp.maximum(b * sb + i * hb - 1, 0), 0)),
            pl.BlockSpec((ts, c), lambda b, i: (b * ns + i, 0)),
            pl.BlockSpec((halo, c), lambda b, i: (jnp.minimum(b * sb + (i + 1) * hb, last), 0)),
            pl.BlockSpec((width, SUBLANES, c), lambda b, i: (0, 0, 0)),
            pl.BlockSpec((1, c), lambda b, i: (0, 0)),
            pl.BlockSpec((1, c), lambda b, i: (0, 0)),
            pl.BlockSpec((1, c), lambda b, i: (0, 0)),
        ],
        out_specs=pl.BlockSpec((ts, c), lambda b, i: (b * ns + i, 0)),
        out_shape=jax.ShapeDtypeStruct((m, c), BF16),
        scratch_shapes=[pltpu.VMEM((SUBLANES, ts + 2 * halo, c), F32), pltpu.VMEM((ts, c), F32)],
        compiler_params=_params(2),
        name="conv",
    )(h, h, h, w8, b_dw, ln_g, ln_b)


def _mixout_body(y_ref, c_ref, wf_ref, wc_ref, ga_ref, gc_ref, wo_ref, r_ref, g_ref, b_ref,
                 o_ref, ob_ref, acc_ref, *, alpha, nj):
    j = pl.program_id(1)

    @pl.when(j == 0)
    def _():
        acc_ref[...] = jnp.zeros_like(acc_ref)

    @pl.when(j < nj)
    def _():
        a = _dot(y_ref[...], wf_ref[...])
        c = _dot(c_ref[...], wc_ref[...])
        mixed = (ga_ref[...] * a + gc_ref[...] * c).astype(BF16)
        acc_ref[...] += _dot(mixed, wo_ref[...])

    @pl.when(j >= nj)
    def _():
        _residual_ln_step(j - nj, acc_ref, r_ref, g_ref, b_ref, o_ref, ob_ref, alpha=alpha)


def _mixout(y, cin, w_fourier, w_conv_out, gates, w_out, x, ln_g, ln_b, *, alpha):
    m, kf = y.shape
    kc = cin.shape[1]
    d = w_out.shape[1]
    tm = _tile(m, 512, LN_STEP_ROWS)
    tn = _tile(d, 512, 128)
    nj = d // tn
    ne = tm // LN_STEP_ROWS
    jw = lambda j: jnp.minimum(j, nj - 1)
    ln_spec = _ln_step_specs(tm, d, nj)
    return pl.pallas_call(
        functools.partial(_mixout_body, alpha=alpha, nj=nj),
        grid=(m // tm, nj + ne),
        in_specs=[
            pl.BlockSpec((tm, kf), lambda i, j: (i, 0)),
            pl.BlockSpec((tm, kc), lambda i, j: (i, 0)),
            pl.BlockSpec((kf, tn), lambda i, j: (0, jw(j))),
            pl.BlockSpec((kc, tn), lambda i, j: (0, jw(j))),
            pl.BlockSpec((tm, tn), lambda i, j: (i, jw(j))),
            pl.BlockSpec((tm, tn), lambda i, j: (i, jw(j) + nj)),
            pl.BlockSpec((tn, d), lambda i, j: (jw(j), 0)),
            ln_spec,
            pl.BlockSpec((1, d), lambda i, j: (0, 0)),
            pl.BlockSpec((1, d), lambda i, j: (0, 0)),
        ],
        out_specs=[ln_spec, ln_spec],
        out_shape=[jax.ShapeDtypeStruct((m, d), F32), jax.ShapeDtypeStruct((m, d), BF16)],
        scratch_shapes=[pltpu.VMEM((tm, d), F32)],
        compiler_params=_params(2),
        name="mixout",
    )(y, cin, w_fourier, w_conv_out, gates, gates, w_out, x, ln_g, ln_b)


def _encoder_layer(x, xb, p, tables, batch, s, alpha):
    chan_table, seq_table = tables
    x, xb = _ffn(x, xb, p["w_ffn1_in"], p["w_ffn1_out"], p["ln_ffn1_g"], p["ln_ffn1_b"], alpha=alpha)
    u, h, gates = _proj(xb, p["w_in"], p["b_in"])
    u3 = u.reshape(batch, s, -1)
    u_mirror = jnp.concatenate([u3[:, :1], jnp.flip(u3[:, s // 2 + 1:], axis=1)], axis=1)
    z, zm = _cdft(u, u_mirror.reshape(batch * (s // 2), -1), chan_table, batch, s)
    y = _sdft(z, zm, seq_table)
    cin = _conv(h, p["w_dw"], p["b_dw"], p["ln_conv_g"], p["ln_conv_b"], batch, s)
    x, xb = _mixout(y, cin, p["w_fourier"], p["w_conv_out"], gates, p["w_out"], x,
                    p["ln_mix_g"], p["ln_mix_b"], alpha=alpha)
    x, xb = _ffn(x, xb, p["w_ffn2_in"], p["w_ffn2_out"], p["ln_ffn2_g"], p["ln_ffn2_b"], alpha=alpha)
    return x, xb


_MATMUL_WEIGHTS = ("w_ffn1_in", "w_ffn1_out", "w_in", "w_fourier", "w_conv_out", "w_out",
                   "w_ffn2_in", "w_ffn2_out")
_HALF_STEP_WEIGHTS = ("w_ffn1_out", "w_ffn2_out")
_ROW_PARAMS = ("ln_ffn1_g", "ln_ffn1_b", "b_in", "b_dw", "ln_conv_g", "ln_conv_b",
               "ln_mix_g", "ln_mix_b", "ln_ffn2_g", "ln_ffn2_b")


def kernel(x_prompt, x_sample, ln_ffn1_g, ln_ffn1_b, w_ffn1_in, w_ffn1_out, w_in, b_in, w_fourier,
           w_dw, b_dw, ln_conv_g, ln_conv_b, w_conv_out, w_out, ln_mix_g, ln_mix_b,
           w_ffn2_in, w_ffn2_out, ln_ffn2_g, ln_ffn2_b):
    stacked = dict(ln_ffn1_g=ln_ffn1_g, ln_ffn1_b=ln_ffn1_b, w_ffn1_in=w_ffn1_in, w_ffn1_out=w_ffn1_out,
                   w_in=w_in, b_in=b_in, w_fourier=w_fourier, w_dw=w_dw, b_dw=b_dw,
                   ln_conv_g=ln_conv_g, ln_conv_b=ln_conv_b, w_conv_out=w_conv_out, w_out=w_out,
                   ln_mix_g=ln_mix_g, ln_mix_b=ln_mix_b, w_ffn2_in=w_ffn2_in, w_ffn2_out=w_ffn2_out,
                   ln_ffn2_g=ln_ffn2_g, ln_ffn2_b=ln_ffn2_b)
    depth = w_ffn1_in.shape[0]
    alpha = float((2 * depth) ** 0.25)
    layers = []
    for l in range(depth):
        p = {k: _cast_layer(stacked[k], l, 0.5 if k in _HALF_STEP_WEIGHTS else 1.0) for k in _MATMUL_WEIGHTS}
        p.update({k: stacked[k][l].reshape(1, -1) for k in _ROW_PARAMS})
        p["w_dw"] = w_dw[l]
        layers.append(p)

    cg = w_fourier.shape[1] // N_FOURIER_GROUPS

    def trunk(x):
        batch, s, d = x.shape
        tables = _dft_tables(s, cg)
        x = x.reshape(batch * s, d)
        xb = x.astype(BF16)
        for p in layers:
            x, xb = _encoder_layer(x, xb, p, tables, batch, s, alpha)
        return x.reshape(batch, s, d)

    return trunk(x_prompt), trunk(x_sample)
```

```python
import functools
import math

import jax
import jax.numpy as jnp
from jax import lax
from jax.experimental import pallas as pl
from jax.experimental.pallas import tpu as pltpu

LN_EPS = 1e-5
N_FOURIER_GROUPS = 4
SUBLANES = 8
CONV_HALO_ROWS = 16
MID_ROWS = 16
LN_STEP_ROWS = 128
LN_UNROLL = 8
V7X_VMEM_LIMIT_BYTES = 56 * 1024 * 1024
BF16 = jnp.bfloat16
F32 = jnp.float32


def _params(n_axes):
    return pltpu.CompilerParams(dimension_semantics=("arbitrary",) * n_axes,
                                vmem_limit_bytes=V7X_VMEM_LIMIT_BYTES)


def _tile(n, pref, align):
    if n <= pref:
        return n
    t = (pref // align) * align
    while t > align and n % t:
        t -= align
    assert n % t == 0, (n, pref, align)
    return t


def _dot(a, b):
    return jnp.dot(a, b, preferred_element_type=F32)


def _layer_norm(y, g, b):
    mu = jnp.mean(y, axis=-1, keepdims=True)
    d = y - mu
    var = jnp.mean(d * d, axis=-1, keepdims=True)
    return d * lax.rsqrt(var + LN_EPS) * g + b


def _for_row_chunks(n_rows, chunk, fn, unroll=1):
    def body(c, carry):
        fn(pl.multiple_of(c * chunk, chunk))
        return carry

    lax.fori_loop(0, n_rows // chunk, body, 0, unroll=unroll)


def _residual_ln_step(e, acc_ref, r_ref, g_ref, b_ref, o_ref, ob_ref, *, alpha):
    step_rows = r_ref.shape[0]
    base = pl.multiple_of(e * step_rows, step_rows)

    def rows_fn(r0):
        rows = pl.ds(r0, SUBLANES)
        y = alpha * r_ref[rows, :] + acc_ref[pl.ds(base + r0, SUBLANES), :]
        z = _layer_norm(y, g_ref[...], b_ref[...])
        o_ref[rows, :] = z
        ob_ref[rows, :] = z.astype(BF16)

    _for_row_chunks(step_rows, SUBLANES, rows_fn, unroll=LN_UNROLL)


def _ln_step_specs(tm, d, nj):
    per_block = tm // LN_STEP_ROWS
    idx = lambda i, j: (i * per_block + jnp.maximum(j - nj, 0), 0)
    return pl.BlockSpec((LN_STEP_ROWS, d), idx)


CAST_BLOCK_BYTES = 4 * 1024 * 1024


def _cast_body(w_ref, o_ref, *, scale):
    w = w_ref[...]
    if scale != 1.0:
        w = w * scale
    o_ref[...] = w.astype(BF16)


def _cast_layer(w_stacked, layer, scale=1.0):
    _, r, c = w_stacked.shape
    tr = _tile(r, max(16, CAST_BLOCK_BYTES // (4 * c) // 16 * 16), 16)
    return pl.pallas_call(
        functools.partial(_cast_body, scale=scale),
        grid=(r // tr,),
        in_specs=[pl.BlockSpec((None, tr, c), lambda i: (layer, i, 0))],
        out_specs=pl.BlockSpec((tr, c), lambda i: (i, 0)),
        out_shape=jax.ShapeDtypeStruct((r, c), BF16),
        compiler_params=_params(1),
        name="cast",
    )(w_stacked)


def _ffn_body(xb_ref, wg_ref, wu_ref, wo_ref, r_ref, g_ref, b_ref, o_ref, ob_ref, acc_ref, *, alpha, nj):
    j = pl.program_id(1)

    @pl.when(j == 0)
    def _():
        acc_ref[...] = jnp.zeros_like(acc_ref)

    @pl.when(j < nj)
    def _():
        xb = xb_ref[...]
        gate = _dot(xb, wg_ref[...])
        up = _dot(xb, wu_ref[...])
        h = (gate * jax.nn.sigmoid(gate) * up).astype(BF16)
        acc_ref[...] += _dot(h, wo_ref[...])

    @pl.when(j >= nj)
    def _():
        _residual_ln_step(j - nj, acc_ref, r_ref, g_ref, b_ref, o_ref, ob_ref, alpha=alpha)


def _ffn(x, xb, w_in, w_out, ln_g, ln_b, *, alpha):
    m, d = x.shape
    f = w_out.shape[0]
    tm = _tile(m, 1024, LN_STEP_ROWS)
    tf = _tile(f, 256, 128)
    nj = f // tf
    ne = tm // LN_STEP_ROWS
    jw = lambda j: jnp.minimum(j, nj - 1)
    ln_spec = _ln_step_specs(tm, d, nj)
    return pl.pallas_call(
        functools.partial(_ffn_body, alpha=alpha, nj=nj),
        grid=(m // tm, nj + ne),
        in_specs=[
            pl.BlockSpec((tm, d), lambda i, j: (i, 0), pipeline_mode=pl.Buffered(1)),
            pl.BlockSpec((d, tf), lambda i, j: (0, jw(j))),
            pl.BlockSpec((d, tf), lambda i, j: (0, jw(j) + nj)),
            pl.BlockSpec((tf, d), lambda i, j: (jw(j), 0)),
            ln_spec,
            pl.BlockSpec((1, d), lambda i, j: (0, 0)),
            pl.BlockSpec((1, d), lambda i, j: (0, 0)),
        ],
        out_specs=[ln_spec, ln_spec],
        out_shape=[jax.ShapeDtypeStruct((m, d), F32), jax.ShapeDtypeStruct((m, d), BF16)],
        scratch_shapes=[pltpu.VMEM((tm, d), F32)],
        compiler_params=_params(2),
        name="ffn",
    )(xb, w_in, w_in, w_out, x, ln_g, ln_b)


def _proj_body(xb_ref, w_ref, b_ref, wv_ref, bv_ref, wg_ref, bg_ref, u_ref, h_ref, gt_ref, *, ng):
    j = pl.program_id(1)

    @pl.when(j < ng)
    def _():
        u_ref[...] = (_dot(xb_ref[...], w_ref[...]) + b_ref[...]).astype(u_ref.dtype)

    @pl.when((j >= ng) & (j < 2 * ng))
    def _():
        xb = xb_ref[...]
        val = _dot(xb, wv_ref[...]) + bv_ref[...]
        gate = _dot(xb, wg_ref[...]) + bg_ref[...]
        h_ref[...] = val * jax.nn.sigmoid(gate)

    @pl.when(j >= 2 * ng)
    def _():
        gt_ref[...] = jax.nn.sigmoid(_dot(xb_ref[...], w_ref[...]) + b_ref[...]).astype(gt_ref.dtype)


def _proj(xb, w_in, b_in):
    m, d = xb.shape
    ng = N_FOURIER_GROUPS
    tn = d // (2 * ng)
    assert w_in.shape[1] == 7 * ng * tn
    tm = _tile(m, 1024, 8)
    main = lambda i, j: (0, jnp.where(j < 2 * ng, jnp.minimum(j, ng - 1), j + ng))
    val = lambda i, j: (0, ng + jnp.clip(j - ng, 0, ng - 1))
    gate = lambda i, j: (0, 2 * ng + jnp.clip(j - ng, 0, ng - 1))
    return pl.pallas_call(
        functools.partial(_proj_body, ng=ng),
        grid=(m // tm, 6 * ng),
        in_specs=[
            pl.BlockSpec((tm, d), lambda i, j: (i, 0), pipeline_mode=pl.Buffered(1)),
            pl.BlockSpec((d, tn), main),
            pl.BlockSpec((1, tn), main),
            pl.BlockSpec((d, tn), val),
            pl.BlockSpec((1, tn), val),
            pl.BlockSpec((d, tn), gate),
            pl.BlockSpec((1, tn), gate),
        ],
        out_specs=[
            pl.BlockSpec((tm, tn), lambda i, j: (i, jnp.minimum(j, ng - 1))),
            pl.BlockSpec((tm, tn), lambda i, j: (i, jnp.clip(j - ng, 0, ng - 1))),
            pl.BlockSpec((tm, tn), lambda i, j: (i, jnp.clip(j - 2 * ng, 0, 4 * ng - 1))),
        ],
        out_shape=[
            jax.ShapeDtypeStruct((m, ng * tn), BF16),
            jax.ShapeDtypeStruct((m, ng * tn), F32),
            jax.ShapeDtypeStruct((m, 4 * ng * tn), F32),
        ],
        compiler_params=_params(2),
        name="proj",
    )(xb, w_in, b_in, w_in, b_in, w_in, b_in)


def _cos_sin(rows, cols, period):
    q = 1
    while q * q * 4 <= rows and rows % (2 * q) == 0:
        q *= 2
    col = jnp.arange(cols, dtype=jnp.int32)[None, :]

    def part(row_values):
        r = (row_values[:, None] * col) % period
        ang = r.astype(F32) * (2.0 * math.pi / period)
        return jnp.cos(ang), jnp.sin(ang)

    ca, sa = part(jnp.arange(rows // q, dtype=jnp.int32) * q)
    cb, sb = part(jnp.arange(q, dtype=jnp.int32))
    scale = 1.0 / math.sqrt(period)
    ca, sa = (ca * scale)[:, None, :], (sa * scale)[:, None, :]
    cb, sb = cb[None, :, :], sb[None, :, :]
    cos = (ca * cb - sa * sb).reshape(rows, cols)
    sin = (sa * cb + ca * sb).reshape(rows, cols)
    return cos, sin


def _dft_tables(s, cg):
    cc, sc = _cos_sin(cg, cg, cg)
    cs, ss = _cos_sin(s, s // 2, s)
    cs = cs * jnp.where(jnp.arange(s // 2) == 0, 0.5, 1.0)[None, :]
    chan = jnp.concatenate([cc, sc], axis=1).astype(BF16)
    seq = jnp.concatenate([cs, -ss], axis=1).astype(BF16)
    return chan, seq


def _cdft_body(u_ref, ur_ref, u0_ref, um_ref, p_ref, t_ref, z_ref, zm_ref, *, ng, cg):
    t = t_ref[...]
    row = lax.broadcasted_iota(jnp.int32, (u_ref.shape[0], 1), 0)
    for g in range(ng):
        cols = slice(g * cg, (g + 1) * cg)
        mirror = jnp.where(row == 0, u0_ref[0:1, cols].astype(F32), _dot(p_ref[...], ur_ref[:, cols]))
        r = _dot(u_ref[:, cols], t)
        rf = _dot(mirror.astype(BF16), t)
        z_ref[0, 0, :, cols] = (r[:, :cg] + rf[:, :cg]).astype(z_ref.dtype)
        z_ref[0, 1, :, cols] = (r[:, cg:] - rf[:, cg:]).astype(z_ref.dtype)
        zm_ref[0, :, cols] = _dot(um_ref[:, cols], t)[:, :cg]


def _cdft(u, chan_table, batch, s):
    m, c = u.shape
    ng = N_FOURIER_GROUPS
    cg = c // ng
    h = s // 2
    assert h % MID_ROWS == 0
    tm = _tile(h, 1024, MID_ROWS)
    nh = h // tm
    idx = jnp.arange(tm, dtype=jnp.int32)
    reverse = (idx[:, None] + idx[None, :] == tm).astype(BF16)
    first = lambda b, i: ((b * s + jnp.where(i == 0, 0, s - i * tm)) // MID_ROWS, 0)
    return pl.pallas_call(
        functools.partial(_cdft_body, ng=ng, cg=cg),
        grid=(batch, nh),
        in_specs=[
            pl.BlockSpec((tm, c), lambda b, i: (b * 2 * nh + i, 0)),
            pl.BlockSpec((tm, c), lambda b, i: (b * 2 * nh + 2 * nh - 1 - i, 0)),
            pl.BlockSpec((MID_ROWS, c), first),
            pl.BlockSpec((MID_ROWS, c), lambda b, i: ((b * s + h) // MID_ROWS, 0)),
            pl.BlockSpec((tm, tm), lambda b, i: (0, 0)),
            pl.BlockSpec((cg, 2 * cg), lambda b, i: (0, 0)),
        ],
        out_specs=[
            pl.BlockSpec((1, 2, tm, c), lambda b, i: (b, 0, i, 0)),
            pl.BlockSpec((1, MID_ROWS, c), lambda b, i: (b, 0, 0)),
        ],
        out_shape=[
            jax.ShapeDtypeStruct((batch, 2, h, c), BF16),
            jax.ShapeDtypeStruct((batch, MID_ROWS, c), F32),
        ],
        compiler_params=_params(2),
        name="cdft",
    )(u, u, u, u, reverse, chan_table)


def _sdft_body(t_ref, z_ref, zm_ref, y_ref, acc_ref, *, nk, mid_scale):
    k = pl.program_id(2)

    @pl.when(k == 0)
    def _():
        acc_ref[...] = jnp.zeros_like(acc_ref)

    acc_ref[...] += _dot(t_ref[...], z_ref[0])

    @pl.when(k == nk - 1)
    def _():
        row = lax.broadcasted_iota(jnp.int32, (acc_ref.shape[0], 1), 0)
        sign = (1 - 2 * (row & 1)).astype(F32)
        y_ref[...] = (acc_ref[...] + sign * (zm_ref[0, 0:1, :] * mid_scale)).astype(y_ref.dtype)


def _sdft(z, zm, seq_table):
    batch, _, h, c = z.shape
    s = 2 * h
    z = z.reshape(batch, s, c)
    tm = _tile(s, 1024, 16)
    tk = _tile(s, 2048, 128)
    ns, nk = s // tm, s // tk
    return pl.pallas_call(
        functools.partial(_sdft_body, nk=nk, mid_scale=1.0 / math.sqrt(s)),
        grid=(batch, ns, nk),
        in_specs=[
            pl.BlockSpec((tm, tk), lambda b, i, k: (i, k)),
            pl.BlockSpec((1, tk, c), lambda b, i, k: (b, k, 0)),
            pl.BlockSpec((1, MID_ROWS, c), lambda b, i, k: (b, 0, 0)),
        ],
        out_specs=pl.BlockSpec((tm, c), lambda b, i, k: (b * ns + i, 0)),
        out_shape=jax.ShapeDtypeStruct((batch * s, c), BF16),
        scratch_shapes=[pltpu.VMEM((tm, c), F32)],
        compiler_params=_params(3),
        name="sdft",
    )(seq_table, z, zm)


def _conv_body(hp_ref, hm_ref, hn_ref, w8_ref, bdw_ref, g_ref, b_ref, o_ref, sh_ref, y_ref,
               *, ts, ns, width, lc, rc):
    i = pl.program_id(1)
    halo = CONV_HALO_ROWS
    c = hm_ref.shape[1]
    span = ts + 2 * halo - SUBLANES
    sh_ref[0, 0:halo, :] = jnp.where(i > 0, hp_ref[...], 0.0)
    sh_ref[0, halo:halo + ts, :] = hm_ref[...]
    sh_ref[0, halo + ts:halo + ts + halo, :] = jnp.where(i < ns - 1, hn_ref[...], 0.0)
    piece = 7 * SUBLANES
    for s in range(1, SUBLANES):
        for c0 in range(0, c, lc):
            for r0 in range(0, span, piece):
                n = min(piece, span - r0)
                sh_ref[s, r0:r0 + n, c0:c0 + lc] = sh_ref[0, r0 + s:r0 + s + n, c0:c0 + lc]

    off = halo - width // 2
    nq = rc // SUBLANES
    for c0 in range(0, c, lc):
        lanes = slice(c0, c0 + lc)

        def row_chunk(r0, lanes=lanes):
            acc = [None] * nq
            for t in range(width):
                a, s = divmod(t + off, SUBLANES)
                wv = w8_ref[t, :, lanes]
                for q in range(nq):
                    term = wv * sh_ref[s, pl.ds(r0 + SUBLANES * (q + a), SUBLANES), lanes]
                    acc[q] = term if acc[q] is None else acc[q] + term
            for q in range(nq):
                y_ref[pl.ds(r0 + SUBLANES * q, SUBLANES), lanes] = acc[q] + bdw_ref[:, lanes]

        _for_row_chunks(ts, rc, row_chunk)

    def finish(r0):
        rows = pl.ds(r0, SUBLANES)
        z = _layer_norm(y_ref[rows, :], g_ref[...], b_ref[...])
        o_ref[rows, :] = (z * jax.nn.sigmoid(z)).astype(o_ref.dtype)

    _for_row_chunks(ts, SUBLANES, finish, unroll=8)


def _conv(h, w_dw, b_dw, ln_g, ln_b, batch, s):
    m, c = h.shape
    width = w_dw.shape[0]
    halo = CONV_HALO_ROWS
    assert width // 2 <= halo and s % halo == 0
    ts = _tile(s, 256, 4 * halo)
    ns = s // ts
    rc = 4 * SUBLANES
    lc = _tile(c, 512, 128)
    hb, sb = ts // halo, s // halo
    last = m // halo - 1
    w8 = jnp.broadcast_to(w_dw[:, None, :], (width, SUBLANES, c))
    return pl.pallas_call(
        functools.partial(_conv_body, ts=ts, ns=ns, width=width, lc=lc, rc=rc),
        grid=(batch, ns),
        in_specs=[
            pl.BlockSpec((halo, c), lambda b, i: (jnp.maximum(b * sb + i * hb - 1, 0), 0)),
            pl.BlockSpec((ts, c), lambda b, i: (b * ns + i, 0)),
            pl.BlockSpec((halo, c), lambda b, i: (jnp.minimum(b * sb + (i + 1) * hb, last), 0)),
            pl.BlockSpec((width, SUBLANES, c), lambda b, i: (0, 0, 0)),
            pl.BlockSpec((1, c), lambda b, i: (0, 0)),
            pl.BlockSpec((1, c), lambda b, i: (0, 0)),
            pl.BlockSpec((1, c), lambda b, i: (0, 0)),
        ],
        out_specs=pl.BlockSpec((ts, c), lambda b, i: (b * ns + i, 0)),
        out_shape=jax.ShapeDtypeStruct((m, c), BF16),
        scratch_shapes=[pltpu.VMEM((SUBLANES, ts + 2 * halo, c), F32), pltpu.VMEM((ts, c), F32)],
        compiler_params=_params(2),
        name="conv",
    )(h, h, h, w8, b_dw, ln_g, ln_b)


def _mixout_body(y_ref, c_ref, wf_ref, wc_ref, ga_ref, gc_ref, wo_ref, r_ref, g_ref, b_ref,
                 o_ref, ob_ref, acc_ref, *, alpha, nj):
    j = pl.program_id(1)

    @pl.when(j == 0)
    def _():
        acc_ref[...] = jnp.zeros_like(acc_ref)

    @pl.when(j < nj)
    def _():
        a = _dot(y_ref[...], wf_ref[...])
        c = _dot(c_ref[...], wc_ref[...])
        mixed = (ga_ref[...] * a + gc_ref[...] * c).astype(BF16)
        acc_ref[...] += _dot(mixed, wo_ref[...])

    @pl.when(j >= nj)
    def _():
        _residual_ln_step(j - nj, acc_ref, r_ref, g_ref, b_ref, o_ref, ob_ref, alpha=alpha)


def _mixout(y, cin, w_fourier, w_conv_out, gates, w_out, x, ln_g, ln_b, *, alpha):
    m, kf = y.shape
    kc = cin.shape[1]
    d = w_out.shape[1]
    tm = _tile(m, 512, LN_STEP_ROWS)
    tn = _tile(d, 512, 128)
    nj = d // tn
    ne = tm // LN_STEP_ROWS
    jw = lambda j: jnp.minimum(j, nj - 1)
    ln_spec = _ln_step_specs(tm, d, nj)
    return pl.pallas_call(
        functools.partial(_mixout_body, alpha=alpha, nj=nj),
        grid=(m // tm, nj + ne),
        in_specs=[
            pl.BlockSpec((tm, kf), lambda i, j: (i, 0)),
            pl.BlockSpec((tm, kc), lambda i, j: (i, 0)),
            pl.BlockSpec((kf, tn), lambda i, j: (0, jw(j))),
            pl.BlockSpec((kc, tn), lambda i, j: (0, jw(j))),
            pl.BlockSpec((tm, tn), lambda i, j: (i, jw(j))),
            pl.BlockSpec((tm, tn), lambda i, j: (i, jw(j) + nj)),
            pl.BlockSpec((tn, d), lambda i, j: (jw(j), 0)),
            ln_spec,
            pl.BlockSpec((1, d), lambda i, j: (0, 0)),
            pl.BlockSpec((1, d), lambda i, j: (0, 0)),
        ],
        out_specs=[ln_spec, ln_spec],
        out_shape=[jax.ShapeDtypeStruct((m, d), F32), jax.ShapeDtypeStruct((m, d), BF16)],
        scratch_shapes=[pltpu.VMEM((tm, d), F32)],
        compiler_params=_params(2),
        name="mixout",
    )(y, cin, w_fourier, w_conv_out, gates, gates, w_out, x, ln_g, ln_b)


def _encoder_layer(x, xb, p, tables, batch, s, alpha):
    chan_table, seq_table = tables
    x, xb = _ffn(x, xb, p["w_ffn1_in"], p["w_ffn1_out"], p["ln_ffn1_g"], p["ln_ffn1_b"], alpha=alpha)
    u, h, gates = _proj(xb, p["w_in"], p["b_in"])
    z, zm = _cdft(u, chan_table, batch, s)
    y = _sdft(z, zm, seq_table)
    cin = _conv(h, p["w_dw"], p["b_dw"], p["ln_conv_g"], p["ln_conv_b"], batch, s)
    x, xb = _mixout(y, cin, p["w_fourier"], p["w_conv_out"], gates, p["w_out"], x,
                    p["ln_mix_g"], p["ln_mix_b"], alpha=alpha)
    x, xb = _ffn(x, xb, p["w_ffn2_in"], p["w_ffn2_out"], p["ln_ffn2_g"], p["ln_ffn2_b"], alpha=alpha)
    return x, xb


_MATMUL_WEIGHTS = ("w_ffn1_in", "w_ffn1_out", "w_in", "w_fourier", "w_conv_out", "w_out",
                   "w_ffn2_in", "w_ffn2_out")
_HALF_STEP_WEIGHTS = ("w_ffn1_out", "w_ffn2_out")
_ROW_PARAMS = ("ln_ffn1_g", "ln_ffn1_b", "b_in", "b_dw", "ln_conv_g", "ln_conv_b",
               "ln_mix_g", "ln_mix_b", "ln_ffn2_g", "ln_ffn2_b")


def kernel(x_prompt, x_sample, ln_ffn1_g, ln_ffn1_b, w_ffn1_in, w_ffn1_out, w_in, b_in, w_fourier,
           w_dw, b_dw, ln_conv_g, ln_conv_b, w_conv_out, w_out, ln_mix_g, ln_mix_b,
           w_ffn2_in, w_ffn2_out, ln_ffn2_g, ln_ffn2_b):
    stacked = dict(ln_ffn1_g=ln_ffn1_g, ln_ffn1_b=ln_ffn1_b, w_ffn1_in=w_ffn1_in, w_ffn1_out=w_ffn1_out,
                   w_in=w_in, b_in=b_in, w_fourier=w_fourier, w_dw=w_dw, b_dw=b_dw,
                   ln_conv_g=ln_conv_g, ln_conv_b=ln_conv_b, w_conv_out=w_conv_out, w_out=w_out,
                   ln_mix_g=ln_mix_g, ln_mix_b=ln_mix_b, w_ffn2_in=w_ffn2_in, w_ffn2_out=w_ffn2_out,
                   ln_ffn2_g=ln_ffn2_g, ln_ffn2_b=ln_ffn2_b)
    depth = w_ffn1_in.shape[0]
    alpha = float((2 * depth) ** 0.25)
    layers = []
    for l in range(depth):
        p = {k: _cast_layer(stacked[k], l, 0.5 if k in _HALF_STEP_WEIGHTS else 1.0) for k in _MATMUL_WEIGHTS}
        p.update({k: stacked[k][l].reshape(1, -1) for k in _ROW_PARAMS})
        p["w_dw"] = w_dw[l]
        layers.append(p)

    cg = w_fourier.shape[1] // N_FOURIER_GROUPS

    def trunk(x):
        batch, s, d = x.shape
        tables = _dft_tables(s, cg)
        x = x.reshape(batch * s, d)
        xb = x.astype(BF16)
        for p in layers:
            x, xb = _encoder_layer(x, xb, p, tables, batch, s, alpha)
        return x.reshape(batch, s, d)

    return trunk(x_prompt), trunk(x_sample)
```

```python
import functools
import math

import jax
import jax.numpy as jnp
from jax import lax
from jax.experimental import pallas as pl
from jax.experimental.pallas import tpu as pltpu

LN_EPS = 1e-5
N_FOURIER_GROUPS = 4
SUBLANES = 8
CONV_HALO_ROWS = 16
MID_ROWS = 16
LN_STEP_ROWS = 128
LN_UNROLL = 8
V7X_VMEM_LIMIT_BYTES = 56 * 1024 * 1024
BF16 = jnp.bfloat16
F32 = jnp.float32


def _params(n_axes):
    return pltpu.CompilerParams(dimension_semantics=("arbitrary",) * n_axes,
                                vmem_limit_bytes=V7X_VMEM_LIMIT_BYTES)


def _tile(n, pref, align):
    if n <= pref:
        return n
    t = (pref // align) * align
    while t > align and n % t:
        t -= align
    assert n % t == 0, (n, pref, align)
    return t


def _dot(a, b):
    return jnp.dot(a, b, preferred_element_type=F32)


def _layer_norm(y, g, b):
    mu = jnp.mean(y, axis=-1, keepdims=True)
    d = y - mu
    var = jnp.mean(d * d, axis=-1, keepdims=True)
    return d * lax.rsqrt(var + LN_EPS) * g + b


def _for_row_chunks(n_rows, chunk, fn, unroll=1):
    def body(c, carry):
        fn(pl.multiple_of(c * chunk, chunk))
        return carry

    lax.fori_loop(0, n_rows // chunk, body, 0, unroll=unroll)


def _residual_ln_step(e, acc_ref, r_ref, g_ref, b_ref, o_ref, ob_ref, *, alpha):
    step_rows = r_ref.shape[0]
    base = pl.multiple_of(e * step_rows, step_rows)

    def rows_fn(r0):
        rows = pl.ds(r0, SUBLANES)
        y = alpha * r_ref[rows, :] + acc_ref[pl.ds(base + r0, SUBLANES), :]
        z = _layer_norm(y, g_ref[...], b_ref[...])
        o_ref[rows, :] = z
        ob_ref[rows, :] = z.astype(BF16)

    _for_row_chunks(step_rows, SUBLANES, rows_fn, unroll=LN_UNROLL)


def _ln_step_specs(tm, d, nj, step_rows=LN_STEP_ROWS):
    per_block = tm // step_rows
    idx = lambda i, j: (i * per_block + jnp.maximum(j - nj, 0), 0)
    return pl.BlockSpec((step_rows, d), idx)


CAST_BLOCK_BYTES = 4 * 1024 * 1024


def _cast_body(w_ref, o_ref, *, scale):
    w = w_ref[...]
    if scale != 1.0:
        w = w * scale
    o_ref[...] = w.astype(BF16)


def _cast_layer(w_stacked, layer, scale=1.0):
    _, r, c = w_stacked.shape
    tr = _tile(r, max(16, CAST_BLOCK_BYTES // (4 * c) // 16 * 16), 16)
    return pl.pallas_call(
        functools.partial(_cast_body, scale=scale),
        grid=(r // tr,),
        in_specs=[pl.BlockSpec((None, tr, c), lambda i: (layer, i, 0))],
        out_specs=pl.BlockSpec((tr, c), lambda i: (i, 0)),
        out_shape=jax.ShapeDtypeStruct((r, c), BF16),
        compiler_params=_params(1),
        name="cast",
    )(w_stacked)


def _ffn_body(xb_ref, wg_ref, wu_ref, wo_ref, r_ref, g_ref, b_ref, o_ref, ob_ref, acc_ref, *, alpha, nj):
    j = pl.program_id(1)

    @pl.when(j == 0)
    def _():
        acc_ref[...] = jnp.zeros_like(acc_ref)

    @pl.when(j < nj)
    def _():
        xb = xb_ref[...]
        gate = _dot(xb, wg_ref[...])
        up = _dot(xb, wu_ref[...])
        h = (gate * jax.nn.sigmoid(gate) * up).astype(BF16)
        acc_ref[...] += _dot(h, wo_ref[...])

    @pl.when(j >= nj)
    def _():
        _residual_ln_step(j - nj, acc_ref, r_ref, g_ref, b_ref, o_ref, ob_ref, alpha=alpha)


def _ffn(x, xb, w_in, w_out, ln_g, ln_b, *, alpha):
    m, d = x.shape
    f = w_out.shape[0]
    tm = _tile(m, 1024, LN_STEP_ROWS)
    tf = _tile(f, 256, 128)
    nj = f // tf
    ne = tm // LN_STEP_ROWS
    jw = lambda j: jnp.minimum(j, nj - 1)
    ln_spec = _ln_step_specs(tm, d, nj)
    return pl.pallas_call(
        functools.partial(_ffn_body, alpha=alpha, nj=nj),
        grid=(m // tm, nj + ne),
        in_specs=[
            pl.BlockSpec((tm, d), lambda i, j: (i, 0), pipeline_mode=pl.Buffered(1)),
            pl.BlockSpec((d, tf), lambda i, j: (0, jw(j))),
            pl.BlockSpec((d, tf), lambda i, j: (0, jw(j) + nj)),
            pl.BlockSpec((tf, d), lambda i, j: (jw(j), 0)),
            ln_spec,
            pl.BlockSpec((1, d), lambda i, j: (0, 0)),
            pl.BlockSpec((1, d), lambda i, j: (0, 0)),
        ],
        out_specs=[ln_spec, ln_spec],
        out_shape=[jax.ShapeDtypeStruct((m, d), F32), jax.ShapeDtypeStruct((m, d), BF16)],
        scratch_shapes=[pltpu.VMEM((tm, d), F32)],
        compiler_params=_params(2),
        name="ffn",
    )(xb, w_in, w_in, w_out, x, ln_g, ln_b)


def _proj_body(xb_ref, w_ref, b_ref, wv_ref, bv_ref, wg_ref, bg_ref, u_ref, h_ref, gt_ref, *, ng):
    j = pl.program_id(1)

    @pl.when(j < ng)
    def _():
        u_ref[...] = (_dot(xb_ref[...], w_ref[...]) + b_ref[...]).astype(u_ref.dtype)

    @pl.when((j >= ng) & (j < 2 * ng))
    def _():
        xb = xb_ref[...]
        val = _dot(xb, wv_ref[...]) + bv_ref[...]
        gate = _dot(xb, wg_ref[...]) + bg_ref[...]
        h_ref[...] = val * jax.nn.sigmoid(gate)

    @pl.when(j >= 2 * ng)
    def _():
        gt_ref[...] = jax.nn.sigmoid(_dot(xb_ref[...], w_ref[...]) + b_ref[...]).astype(gt_ref.dtype)


def _proj(xb, w_in, b_in):
    m, d = xb.shape
    ng = N_FOURIER_GROUPS
    tn = d // (2 * ng)
    assert w_in.shape[1] == 7 * ng * tn
    tm = _tile(m, 1024, 8)
    main = lambda i, j: (0, jnp.where(j < 2 * ng, jnp.minimum(j, ng - 1), j + ng))
    val = lambda i, j: (0, ng + jnp.clip(j - ng, 0, ng - 1))
    gate = lambda i, j: (0, 2 * ng + jnp.clip(j - ng, 0, ng - 1))
    return pl.pallas_call(
        functools.partial(_proj_body, ng=ng),
        grid=(m // tm, 6 * ng),
        in_specs=[
            pl.BlockSpec((tm, d), lambda i, j: (i, 0), pipeline_mode=pl.Buffered(1)),
            pl.BlockSpec((d, tn), main),
            pl.BlockSpec((1, tn), main),
            pl.BlockSpec((d, tn), val),
            pl.BlockSpec((1, tn), val),
            pl.BlockSpec((d, tn), gate),
            pl.BlockSpec((1, tn), gate),
        ],
        out_specs=[
            pl.BlockSpec((tm, tn), lambda i, j: (i, jnp.minimum(j, ng - 1))),
            pl.BlockSpec((tm, tn), lambda i, j: (i, jnp.clip(j - ng, 0, ng - 1))),
            pl.BlockSpec((tm, tn), lambda i, j: (i, jnp.clip(j - 2 * ng, 0, 4 * ng - 1))),
        ],
        out_shape=[
            jax.ShapeDtypeStruct((m, ng * tn), BF16),
            jax.ShapeDtypeStruct((m, ng * tn), F32),
            jax.ShapeDtypeStruct((m, 4 * ng * tn), BF16),
        ],
        compiler_params=_params(2),
        name="proj",
    )(xb, w_in, b_in, w_in, b_in, w_in, b_in)


def _cos_sin(rows, cols, period):
    q = 1
    while q * q * 4 <= rows and rows % (2 * q) == 0:
        q *= 2
    col = jnp.arange(cols, dtype=jnp.int32)[None, :]

    def part(row_values):
        r = (row_values[:, None] * col) % period
        ang = r.astype(F32) * (2.0 * math.pi / period)
        return jnp.cos(ang), jnp.sin(ang)

    ca, sa = part(jnp.arange(rows // q, dtype=jnp.int32) * q)
    cb, sb = part(jnp.arange(q, dtype=jnp.int32))
    scale = 1.0 / math.sqrt(period)
    ca, sa = (ca * scale)[:, None, :], (sa * scale)[:, None, :]
    cb, sb = cb[None, :, :], sb[None, :, :]
    cos = (ca * cb - sa * sb).reshape(rows, cols)
    sin = (sa * cb + ca * sb).reshape(rows, cols)
    return cos, sin


def _dft_tables(s, cg):
    cc, sc = _cos_sin(cg, cg, cg)
    cs, ss = _cos_sin(s, s // 2, s)
    cs = cs * jnp.where(jnp.arange(s // 2) == 0, 0.5, 1.0)[None, :]
    chan = jnp.concatenate([cc, sc], axis=1).astype(BF16)
    seq = jnp.concatenate([cs, -ss], axis=1).astype(BF16)
    return chan, seq


def _cdft_body(u_ref, ur_ref, u0_ref, um_ref, p_ref, t_ref, z_ref, zm_ref, *, ng, cg):
    t = t_ref[...]
    row = lax.broadcasted_iota(jnp.int32, (u_ref.shape[0], 1), 0)
    for g in range(ng):
        cols = slice(g * cg, (g + 1) * cg)
        mirror = jnp.where(row == 0, u0_ref[0:1, cols].astype(F32), _dot(p_ref[...], ur_ref[:, cols]))
        r = _dot(u_ref[:, cols], t)
        rf = _dot(mirror.astype(BF16), t)
        z_ref[0, 0, :, cols] = (r[:, :cg] + rf[:, :cg]).astype(z_ref.dtype)
        z_ref[0, 1, :, cols] = (r[:, cg:] - rf[:, cg:]).astype(z_ref.dtype)
        zm_ref[0, :, cols] = _dot(um_ref[:, cols], t)[:, :cg]


def _cdft(u, chan_table, batch, s):
    m, c = u.shape
    ng = N_FOURIER_GROUPS
    cg = c // ng
    h = s // 2
    assert h % MID_ROWS == 0
    tm = _tile(h, 1024, MID_ROWS)
    nh = h // tm
    idx = jnp.arange(tm, dtype=jnp.int32)
    reverse = (idx[:, None] + idx[None, :] == tm).astype(BF16)
    first = lambda b, i: ((b * s + jnp.where(i == 0, 0, s - i * tm)) // MID_ROWS, 0)
    return pl.pallas_call(
        functools.partial(_cdft_body, ng=ng, cg=cg),
        grid=(batch, nh),
        in_specs=[
            pl.BlockSpec((tm, c), lambda b, i: (b * 2 * nh + i, 0)),
            pl.BlockSpec((tm, c), lambda b, i: (b * 2 * nh + 2 * nh - 1 - i, 0)),
            pl.BlockSpec((MID_ROWS, c), first),
            pl.BlockSpec((MID_ROWS, c), lambda b, i: ((b * s + h) // MID_ROWS, 0)),
            pl.BlockSpec((tm, tm), lambda b, i: (0, 0)),
            pl.BlockSpec((cg, 2 * cg), lambda b, i: (0, 0)),
        ],
        out_specs=[
            pl.BlockSpec((1, 2, tm, c), lambda b, i: (b, 0, i, 0)),
            pl.BlockSpec((1, MID_ROWS, c), lambda b, i: (b, 0, 0)),
        ],
        out_shape=[
            jax.ShapeDtypeStruct((batch, 2, h, c), BF16),
            jax.ShapeDtypeStruct((batch, MID_ROWS, c), F32),
        ],
        compiler_params=_params(2),
        name="cdft",
    )(u, u, u, u, reverse, chan_table)


def _sdft_body(t_ref, z_ref, zm_ref, y_ref, acc_ref, *, nk, mid_scale):
    k = pl.program_id(2)

    @pl.when(k == 0)
    def _():
        acc_ref[...] = jnp.zeros_like(acc_ref)

    acc_ref[...] += _dot(t_ref[...], z_ref[0])

    @pl.when(k == nk - 1)
    def _():
        row = lax.broadcasted_iota(jnp.int32, (acc_ref.shape[0], 1), 0)
        sign = (1 - 2 * (row & 1)).astype(F32)
        y_ref[...] = (acc_ref[...] + sign * (zm_ref[0, 0:1, :] * mid_scale)).astype(y_ref.dtype)


def _sdft(z, zm, seq_table):
    batch, _, h, c = z.shape
    s = 2 * h
    z = z.reshape(batch, s, c)
    tm = _tile(s, 1024, 16)
    tk = _tile(s, 2048, 128)
    ns, nk = s // tm, s // tk
    return pl.pallas_call(
        functools.partial(_sdft_body, nk=nk, mid_scale=1.0 / math.sqrt(s)),
        grid=(batch, ns, nk),
        in_specs=[
            pl.BlockSpec((tm, tk), lambda b, i, k: (i, k)),
            pl.BlockSpec((1, tk, c), lambda b, i, k: (b, k, 0)),
            pl.BlockSpec((1, MID_ROWS, c), lambda b, i, k: (b, 0, 0)),
        ],
        out_specs=pl.BlockSpec((tm, c), lambda b, i, k: (b * ns + i, 0)),
        out_shape=jax.ShapeDtypeStruct((batch * s, c), BF16),
        scratch_shapes=[pltpu.VMEM((tm, c), F32)],
        compiler_params=_params(3),
        name="sdft",
    )(seq_table, z, zm)


def _conv_body(hp_ref, hm_ref, hn_ref, w8_ref, bdw_ref, g_ref, b_ref, o_ref, sh_ref, y_ref,
               *, ts, ns, width, lc, rc):
    i = pl.program_id(1)
    halo = CONV_HALO_ROWS
    c = hm_ref.shape[1]
    span = ts + 2 * halo - SUBLANES
    sh_ref[0, 0:halo, :] = jnp.where(i > 0, hp_ref[...], 0.0)
    sh_ref[0, halo:halo + ts, :] = hm_ref[...]
    sh_ref[0, halo + ts:halo + ts + halo, :] = jnp.where(i < ns - 1, hn_ref[...], 0.0)
    piece = 7 * SUBLANES
    for s in range(1, SUBLANES):
        for c0 in range(0, c, lc):
            for r0 in range(0, span, piece):
                n = min(piece, span - r0)
                sh_ref[s, r0:r0 + n, c0:c0 + lc] = sh_ref[0, r0 + s:r0 + s + n, c0:c0 + lc]

    off = halo - width // 2
    nq = rc // SUBLANES
    for c0 in range(0, c, lc):
        lanes = slice(c0, c0 + lc)

        def row_chunk(r0, lanes=lanes):
            acc = [None] * nq
            for t in range(width):
                a, s = divmod(t + off, SUBLANES)
                wv = w8_ref[t, :, lanes]
                for q in range(nq):
                    term = wv * sh_ref[s, pl.ds(r0 + SUBLANES * (q + a), SUBLANES), lanes]
                    acc[q] = term if acc[q] is None else acc[q] + term
            for q in range(nq):
                y_ref[pl.ds(r0 + SUBLANES * q, SUBLANES), lanes] = acc[q] + bdw_ref[:, lanes]

        _for_row_chunks(ts, rc, row_chunk)

    def finish(r0):
        rows = pl.ds(r0, SUBLANES)
        z = _layer_norm(y_ref[rows, :], g_ref[...], b_ref[...])
        o_ref[rows, :] = (z * jax.nn.sigmoid(z)).astype(o_ref.dtype)

    _for_row_chunks(ts, SUBLANES, finish, unroll=8)


def _conv(h, w_dw, b_dw, ln_g, ln_b, batch, s):
    m, c = h.shape
    width = w_dw.shape[0]
    halo = CONV_HALO_ROWS
    assert width // 2 <= halo and s % halo == 0
    ts = _tile(s, 256, 4 * halo)
    ns = s // ts
    rc = 4 * SUBLANES
    lc = _tile(c, 512, 128)
    hb, sb = ts // halo, s // halo
    last = m // halo - 1
    w8 = jnp.broadcast_to(w_dw[:, None, :], (width, SUBLANES, c))
    return pl.pallas_call(
        functools.partial(_conv_body, ts=ts, ns=ns, width=width, lc=lc, rc=rc),
        grid=(batch, ns),
        in_specs=[
            pl.BlockSpec((halo, c), lambda b, i: (jnp.maximum(b * sb + i * hb - 1, 0), 0)),
            pl.BlockSpec((ts, c), lambda b, i: (b * ns + i, 0)),
            pl.BlockSpec((halo, c), lambda b, i: (jnp.minimum(b * sb + (i + 1) * hb, last), 0)),
            pl.BlockSpec((width, SUBLANES, c), lambda b, i: (0, 0, 0)),
            pl.BlockSpec((1, c), lambda b, i: (0, 0)),
            pl.BlockSpec((1, c), lambda b, i: (0, 0)),
            pl.BlockSpec((1, c), lambda b, i: (0, 0)),
        ],
        out_specs=pl.BlockSpec((ts, c), lambda b, i: (b * ns + i, 0)),
        out_shape=jax.ShapeDtypeStruct((m, c), BF16),
        scratch_shapes=[pltpu.VMEM((SUBLANES, ts + 2 * halo, c), F32), pltpu.VMEM((ts, c), F32)],
        compiler_params=_params(2),
        name="conv",
    )(h, h, h, w8, b_dw, ln_g, ln_b)


def _mixout_body(y_ref, c_ref, wf_ref, wc_ref, ga_ref, gc_ref, wo_ref, r_ref, g_ref, b_ref,
                 o_ref, ob_ref, acc_ref, *, alpha, nj):
    j = pl.program_id(1)

    @pl.when(j == 0)
    def _():
        acc_ref[...] = jnp.zeros_like(acc_ref)

    @pl.when(j < nj)
    def _():
        a = _dot(y_ref[...], wf_ref[...])
        c = _dot(c_ref[...], wc_ref[...])
        mixed = (ga_ref[...] * a + gc_ref[...] * c).astype(BF16)
        acc_ref[...] += _dot(mixed, wo_ref[...])

    @pl.when(j >= nj)
    def _():
        _residual_ln_step(j - nj, acc_ref, r_ref, g_ref, b_ref, o_ref, ob_ref, alpha=alpha)


def _mixout(y, cin, w_fourier, w_conv_out, gates, w_out, x, ln_g, ln_b, *, alpha):
    m, kf = y.shape
    kc = cin.shape[1]
    d = w_out.shape[1]
    tm = _tile(m, 1024, LN_STEP_ROWS)
    tn = _tile(d, 512, 128)
    nj = d // tn
    ln_rows = LN_STEP_ROWS // 2
    ne = tm // ln_rows
    jw = lambda j: jnp.minimum(j, nj - 1)
    ln_spec = _ln_step_specs(tm, d, nj, ln_rows)
    return pl.pallas_call(
        functools.partial(_mixout_body, alpha=alpha, nj=nj),
        grid=(m // tm, nj + ne),
        in_specs=[
            pl.BlockSpec((tm, kf), lambda i, j: (i, 0), pipeline_mode=pl.Buffered(1)),
            pl.BlockSpec((tm, kc), lambda i, j: (i, 0), pipeline_mode=pl.Buffered(1)),
            pl.BlockSpec((kf, tn), lambda i, j: (0, jw(j))),
            pl.BlockSpec((kc, tn), lambda i, j: (0, jw(j))),
            pl.BlockSpec((tm, tn), lambda i, j: (i, jw(j))),
            pl.BlockSpec((tm, tn), lambda i, j: (i, jw(j) + nj)),
            pl.BlockSpec((tn, d), lambda i, j: (jw(j), 0)),
            ln_spec,
            pl.BlockSpec((1, d), lambda i, j: (0, 0)),
            pl.BlockSpec((1, d), lambda i, j: (0, 0)),
        ],
        out_specs=[ln_spec, ln_spec],
        out_shape=[jax.ShapeDtypeStruct((m, d), F32), jax.ShapeDtypeStruct((m, d), BF16)],
        scratch_shapes=[pltpu.VMEM((tm, d), F32)],
        compiler_params=_params(2),
        name="mixout",
    )(y, cin, w_fourier, w_conv_out, gates, gates, w_out, x, ln_g, ln_b)


def _encoder_layer(x, xb, p, tables, batch, s, alpha):
    chan_table, seq_table = tables
    x, xb = _ffn(x, xb, p["w_ffn1_in"], p["w_ffn1_out"], p["ln_ffn1_g"], p["ln_ffn1_b"], alpha=alpha)
    u, h, gates = _proj(xb, p["w_in"], p["b_in"])
    z, zm = _cdft(u, chan_table, batch, s)
    y = _sdft(z, zm, seq_table)
    cin = _conv(h, p["w_dw"], p["b_dw"], p["ln_conv_g"], p["ln_conv_b"], batch, s)
    x, xb = _mixout(y, cin, p["w_fourier"], p["w_conv_out"], gates, p["w_out"], x,
                    p["ln_mix_g"], p["ln_mix_b"], alpha=alpha)
    x, xb = _ffn(x, xb, p["w_ffn2_in"], p["w_ffn2_out"], p["ln_ffn2_g"], p["ln_ffn2_b"], alpha=alpha)
    return x, xb


_MATMUL_WEIGHTS = ("w_ffn1_in", "w_ffn1_out", "w_in", "w_fourier", "w_conv_out", "w_out",
                   "w_ffn2_in", "w_ffn2_out")
_HALF_STEP_WEIGHTS = ("w_ffn1_out", "w_ffn2_out")
_ROW_PARAMS = ("ln_ffn1_g", "ln_ffn1_b", "b_in", "b_dw", "ln_conv_g", "ln_conv_b",
               "ln_mix_g", "ln_mix_b", "ln_ffn2_g", "ln_ffn2_b")


def kernel(x_prompt, x_sample, ln_ffn1_g, ln_ffn1_b, w_ffn1_in, w_ffn1_out, w_in, b_in, w_fourier,
           w_dw, b_dw, ln_conv_g, ln_conv_b, w_conv_out, w_out, ln_mix_g, ln_mix_b,
           w_ffn2_in, w_ffn2_out, ln_ffn2_g, ln_ffn2_b):
    stacked = dict(ln_ffn1_g=ln_ffn1_g, ln_ffn1_b=ln_ffn1_b, w_ffn1_in=w_ffn1_in, w_ffn1_out=w_ffn1_out,
                   w_in=w_in, b_in=b_in, w_fourier=w_fourier, w_dw=w_dw, b_dw=b_dw,
                   ln_conv_g=ln_conv_g, ln_conv_b=ln_conv_b, w_conv_out=w_conv_out, w_out=w_out,
                   ln_mix_g=ln_mix_g, ln_mix_b=ln_mix_b, w_ffn2_in=w_ffn2_in, w_ffn2_out=w_ffn2_out,
                   ln_ffn2_g=ln_ffn2_g, ln_ffn2_b=ln_ffn2_b)
    depth = w_ffn1_in.shape[0]
    alpha = float((2 * depth) ** 0.25)
    layers = []
    for l in range(depth):
        p = {k: _cast_layer(stacked[k], l, 0.5 if k in _HALF_STEP_WEIGHTS else 1.0) for k in _MATMUL_WEIGHTS}
        p.update({k: stacked[k][l].reshape(1, -1) for k in _ROW_PARAMS})
        p["w_dw"] = w_dw[l]
        layers.append(p)

    cg = w_fourier.shape[1] // N_FOURIER_GROUPS

    def trunk(x):
        batch, s, d = x.shape
        tables = _dft_tables(s, cg)
        x = x.reshape(batch * s, d)
        xb = x.astype(BF16)
        for p in layers:
            x, xb = _encoder_layer(x, xb, p, tables, batch, s, alpha)
        return x.reshape(batch, s, d)

    return trunk(x_prompt), trunk(x_sample)
```

```python
import functools
import math

import jax
import jax.numpy as jnp
from jax import lax
from jax.experimental import pallas as pl
from jax.experimental.pallas import tpu as pltpu

LN_EPS = 1e-5
N_FOURIER_GROUPS = 4
SUBLANES = 8
CONV_HALO_ROWS = 16
MID_ROWS = 16
LN_STEP_ROWS = 128
LN_UNROLL = 8
V7X_VMEM_LIMIT_BYTES = 56 * 1024 * 1024
BF16 = jnp.bfloat16
F32 = jnp.float32


def _params(n_axes):
    return pltpu.CompilerParams(dimension_semantics=("arbitrary",) * n_axes,
                                vmem_limit_bytes=V7X_VMEM_LIMIT_BYTES)


def _tile(n, pref, align):
    if n <= pref:
        return n
    t = (pref // align) * align
    while t > align and n % t:
        t -= align
    assert n % t == 0, (n, pref, align)
    return t


def _dot(a, b):
    return jnp.dot(a, b, preferred_element_type=F32)


def _layer_norm(y, g, b):
    mu = jnp.mean(y, axis=-1, keepdims=True)
    d = y - mu
    var = jnp.mean(d * d, axis=-1, keepdims=True)
    return d * lax.rsqrt(var + LN_EPS) * g + b


def _for_row_chunks(n_rows, chunk, fn, unroll=1):
    def body(c, carry):
        fn(pl.multiple_of(c * chunk, chunk))
        return carry

    lax.fori_loop(0, n_rows // chunk, body, 0, unroll=unroll)


def _residual_ln_step(e, acc_ref, r_ref, g_ref, b_ref, o_ref, ob_ref, *, alpha):
    step_rows = r_ref.shape[0]
    base = pl.multiple_of(e * step_rows, step_rows)

    def rows_fn(r0):
        rows = pl.ds(r0, SUBLANES)
        y = alpha * r_ref[rows, :] + acc_ref[pl.ds(base + r0, SUBLANES), :]
        z = _layer_norm(y, g_ref[...], b_ref[...])
        o_ref[rows, :] = z
        ob_ref[rows, :] = z.astype(BF16)

    _for_row_chunks(step_rows, SUBLANES, rows_fn, unroll=LN_UNROLL)


def _ln_step_specs(tm, d, nj):
    per_block = tm // LN_STEP_ROWS
    idx = lambda i, j: (i * per_block + jnp.maximum(j - nj, 0), 0)
    return pl.BlockSpec((LN_STEP_ROWS, d), idx)


CAST_BLOCK_BYTES = 4 * 1024 * 1024


def _cast_body(w_ref, o_ref, *, scale):
    w = w_ref[...]
    if scale != 1.0:
        w = w * scale
    o_ref[...] = w.astype(BF16)


def _cast_layer(w_stacked, layer, scale=1.0):
    _, r, c = w_stacked.shape
    tr = _tile(r, max(16, CAST_BLOCK_BYTES // (4 * c) // 16 * 16), 16)
    return pl.pallas_call(
        functools.partial(_cast_body, scale=scale),
        grid=(r // tr,),
        in_specs=[pl.BlockSpec((None, tr, c), lambda i: (layer, i, 0))],
        out_specs=pl.BlockSpec((tr, c), lambda i: (i, 0)),
        out_shape=jax.ShapeDtypeStruct((r, c), BF16),
        compiler_params=_params(1),
        name="cast",
    )(w_stacked)


FFN_HIDDEN_TILE = 256


def _cast_gate_up_body(g_ref, u_ref, o_ref):
    tf = g_ref.shape[1]
    o_ref[:, :tf] = g_ref[...].astype(BF16)
    o_ref[:, tf:] = u_ref[...].astype(BF16)


def _cast_gate_up(w_stacked, layer):
    _, d, f2 = w_stacked.shape
    tf = _tile(f2 // 2, FFN_HIDDEN_TILE, 128)
    nj = f2 // (2 * tf)
    return pl.pallas_call(
        _cast_gate_up_body,
        grid=(nj,),
        in_specs=[
            pl.BlockSpec((None, d, tf), lambda j: (layer, 0, j)),
            pl.BlockSpec((None, d, tf), lambda j: (layer, 0, j + nj)),
        ],
        out_specs=pl.BlockSpec((None, d, 2 * tf), lambda j: (j, 0, 0)),
        out_shape=jax.ShapeDtypeStruct((nj, d, 2 * tf), BF16),
        compiler_params=_params(1),
        name="cast_gate_up",
    )(w_stacked, w_stacked)


def _ffn_body(xb_ref, wgu_ref, wo_ref, r_ref, g_ref, b_ref, o_ref, ob_ref, acc_ref, *, alpha, nj):
    j = pl.program_id(1)
    tf = wo_ref.shape[0]

    @pl.when(j == 0)
    def _():
        acc_ref[...] = jnp.zeros_like(acc_ref)

    @pl.when(j < nj)
    def _():
        gate_up = _dot(xb_ref[...], wgu_ref[...])
        gate, up = gate_up[:, :tf], gate_up[:, tf:]
        h = (gate * jax.nn.sigmoid(gate) * up).astype(BF16)
        acc_ref[...] += _dot(h, wo_ref[...])

    @pl.when(j >= nj)
    def _():
        _residual_ln_step(j - nj, acc_ref, r_ref, g_ref, b_ref, o_ref, ob_ref, alpha=alpha)


def _ffn(x, xb, w_gate_up, w_out, ln_g, ln_b, *, alpha):
    m, d = x.shape
    nj, _, tf2 = w_gate_up.shape
    tf = tf2 // 2
    assert w_out.shape[0] == nj * tf
    tm = _tile(m, 1024, LN_STEP_ROWS)
    ne = tm // LN_STEP_ROWS
    jw = lambda j: jnp.minimum(j, nj - 1)
    ln_spec = _ln_step_specs(tm, d, nj)
    return pl.pallas_call(
        functools.partial(_ffn_body, alpha=alpha, nj=nj),
        grid=(m // tm, nj + ne),
        in_specs=[
            pl.BlockSpec((tm, d), lambda i, j: (i, 0), pipeline_mode=pl.Buffered(1)),
            pl.BlockSpec((None, d, 2 * tf), lambda i, j: (jw(j), 0, 0)),
            pl.BlockSpec((tf, d), lambda i, j: (jw(j), 0)),
            ln_spec,
            pl.BlockSpec((1, d), lambda i, j: (0, 0)),
            pl.BlockSpec((1, d), lambda i, j: (0, 0)),
        ],
        out_specs=[ln_spec, ln_spec],
        out_shape=[jax.ShapeDtypeStruct((m, d), F32), jax.ShapeDtypeStruct((m, d), BF16)],
        scratch_shapes=[pltpu.VMEM((tm, d), F32)],
        compiler_params=_params(2),
        name="ffn",
    )(xb, w_gate_up, w_out, x, ln_g, ln_b)


def _proj_body(xb_ref, w_ref, b_ref, wv_ref, bv_ref, wg_ref, bg_ref, u_ref, h_ref, gt_ref, *, ng):
    j = pl.program_id(1)

    @pl.when(j < ng)
    def _():
        u_ref[...] = (_dot(xb_ref[...], w_ref[...]) + b_ref[...]).astype(u_ref.dtype)

    @pl.when((j >= ng) & (j < 2 * ng))
    def _():
        xb = xb_ref[...]
        val = _dot(xb, wv_ref[...]) + bv_ref[...]
        gate = _dot(xb, wg_ref[...]) + bg_ref[...]
        h_ref[...] = val * jax.nn.sigmoid(gate)

    @pl.when(j >= 2 * ng)
    def _():
        gt_ref[...] = jax.nn.sigmoid(_dot(xb_ref[...], w_ref[...]) + b_ref[...]).astype(gt_ref.dtype)


def _proj(xb, w_in, b_in):
    m, d = xb.shape
    ng = N_FOURIER_GROUPS
    tn = d // (2 * ng)
    assert w_in.shape[1] == 7 * ng * tn
    tm = _tile(m, 1024, 8)
    main = lambda i, j: (0, jnp.where(j < 2 * ng, jnp.minimum(j, ng - 1), j + ng))
    val = lambda i, j: (0, ng + jnp.clip(j - ng, 0, ng - 1))
    gate = lambda i, j: (0, 2 * ng + jnp.clip(j - ng, 0, ng - 1))
    return pl.pallas_call(
        functools.partial(_proj_body, ng=ng),
        grid=(m // tm, 6 * ng),
        in_specs=[
            pl.BlockSpec((tm, d), lambda i, j: (i, 0), pipeline_mode=pl.Buffered(1)),
            pl.BlockSpec((d, tn), main),
            pl.BlockSpec((1, tn), main),
            pl.BlockSpec((d, tn), val),
            pl.BlockSpec((1, tn), val),
            pl.BlockSpec((d, tn), gate),
            pl.BlockSpec((1, tn), gate),
        ],
        out_specs=[
            pl.BlockSpec((tm, tn), lambda i, j: (i, jnp.minimum(j, ng - 1))),
            pl.BlockSpec((tm, tn), lambda i, j: (i, jnp.clip(j - ng, 0, ng - 1))),
            pl.BlockSpec((tm, tn), lambda i, j: (i, jnp.clip(j - 2 * ng, 0, 4 * ng - 1))),
        ],
        out_shape=[
            jax.ShapeDtypeStruct((m, ng * tn), BF16),
            jax.ShapeDtypeStruct((m, ng * tn), F32),
            jax.ShapeDtypeStruct((m, 4 * ng * tn), F32),
        ],
        compiler_params=_params(2),
        name="proj",
    )(xb, w_in, b_in, w_in, b_in, w_in, b_in)


def _cos_sin(rows, cols, period):
    q = 1
    while q * q * 4 <= rows and rows % (2 * q) == 0:
        q *= 2
    col = jnp.arange(cols, dtype=jnp.int32)[None, :]

    def part(row_values):
        r = (row_values[:, None] * col) % period
        ang = r.astype(F32) * (2.0 * math.pi / period)
        return jnp.cos(ang), jnp.sin(ang)

    ca, sa = part(jnp.arange(rows // q, dtype=jnp.int32) * q)
    cb, sb = part(jnp.arange(q, dtype=jnp.int32))
    scale = 1.0 / math.sqrt(period)
    ca, sa = (ca * scale)[:, None, :], (sa * scale)[:, None, :]
    cb, sb = cb[None, :, :], sb[None, :, :]
    cos = (ca * cb - sa * sb).reshape(rows, cols)
    sin = (sa * cb + ca * sb).reshape(rows, cols)
    return cos, sin


def _dft_tables(s, cg):
    cc, sc = _cos_sin(cg, cg, cg)
    cs, ss = _cos_sin(s, s // 2, s)
    cs = cs * jnp.where(jnp.arange(s // 2) == 0, 0.5, 1.0)[None, :]
    chan = jnp.concatenate([cc, sc], axis=1).astype(BF16)
    seq = jnp.concatenate([cs, -ss], axis=1).astype(BF16)
    return chan, seq


def _cdft_body(u_ref, ur_ref, u0_ref, um_ref, p_ref, t_ref, z_ref, zm_ref, *, ng, cg):
    t = t_ref[...]
    row = lax.broadcasted_iota(jnp.int32, (u_ref.shape[0], 1), 0)
    for g in range(ng):
        cols = slice(g * cg, (g + 1) * cg)
        mirror = jnp.where(row == 0, u0_ref[0:1, cols].astype(F32), _dot(p_ref[...], ur_ref[:, cols]))
        r = _dot(u_ref[:, cols], t)
        rf = _dot(mirror.astype(BF16), t)
        z_ref[0, 0, :, cols] = (r[:, :cg] + rf[:, :cg]).astype(z_ref.dtype)
        z_ref[0, 1, :, cols] = (r[:, cg:] - rf[:, cg:]).astype(z_ref.dtype)
        zm_ref[0, :, cols] = _dot(um_ref[:, cols], t)[:, :cg]


def _cdft(u, chan_table, batch, s):
    m, c = u.shape
    ng = N_FOURIER_GROUPS
    cg = c // ng
    h = s // 2
    assert h % MID_ROWS == 0
    tm = _tile(h, 1024, MID_ROWS)
    nh = h // tm
    idx = jnp.arange(tm, dtype=jnp.int32)
    reverse = (idx[:, None] + idx[None, :] == tm).astype(BF16)
    first = lambda b, i: ((b * s + jnp.where(i == 0, 0, s - i * tm)) // MID_ROWS, 0)
    return pl.pallas_call(
        functools.partial(_cdft_body, ng=ng, cg=cg),
        grid=(batch, nh),
        in_specs=[
            pl.BlockSpec((tm, c), lambda b, i: (b * 2 * nh + i, 0)),
            pl.BlockSpec((tm, c), lambda b, i: (b * 2 * nh + 2 * nh - 1 - i, 0)),
            pl.BlockSpec((MID_ROWS, c), first),
            pl.BlockSpec((MID_ROWS, c), lambda b, i: ((b * s + h) // MID_ROWS, 0)),
            pl.BlockSpec((tm, tm), lambda b, i: (0, 0)),
            pl.BlockSpec((cg, 2 * cg), lambda b, i: (0, 0)),
        ],
        out_specs=[
            pl.BlockSpec((1, 2, tm, c), lambda b, i: (b, 0, i, 0)),
            pl.BlockSpec((1, MID_ROWS, c), lambda b, i: (b, 0, 0)),
        ],
        out_shape=[
            jax.ShapeDtypeStruct((batch, 2, h, c), BF16),
            jax.ShapeDtypeStruct((batch, MID_ROWS, c), F32),
        ],
        compiler_params=_params(2),
        name="cdft",
    )(u, u, u, u, reverse, chan_table)


def _sdft_body(t_ref, z_ref, zm_ref, y_ref, acc_ref, *, nk, mid_scale):
    k = pl.program_id(2)

    @pl.when(k == 0)
    def _():
        acc_ref[...] = jnp.zeros_like(acc_ref)

    acc_ref[...] += _dot(t_ref[...], z_ref[0])

    @pl.when(k == nk - 1)
    def _():
        row = lax.broadcasted_iota(jnp.int32, (acc_ref.shape[0], 1), 0)
        sign = (1 - 2 * (row & 1)).astype(F32)
        y_ref[...] = (acc_ref[...] + sign * (zm_ref[0, 0:1, :] * mid_scale)).astype(y_ref.dtype)


def _sdft(z, zm, seq_table):
    batch, _, h, c = z.shape
    s = 2 * h
    z = z.reshape(batch, s, c)
    tm = _tile(s, 1024, 16)
    tk = _tile(s, 2048, 128)
    ns, nk = s // tm, s // tk
    return pl.pallas_call(
        functools.partial(_sdft_body, nk=nk, mid_scale=1.0 / math.sqrt(s)),
        grid=(batch, ns, nk),
        in_specs=[
            pl.BlockSpec((tm, tk), lambda b, i, k: (i, k)),
            pl.BlockSpec((1, tk, c), lambda b, i, k: (b, k, 0)),
            pl.BlockSpec((1, MID_ROWS, c), lambda b, i, k: (b, 0, 0)),
        ],
        out_specs=pl.BlockSpec((tm, c), lambda b, i, k: (b * ns + i, 0)),
        out_shape=jax.ShapeDtypeStruct((batch * s, c), BF16),
        scratch_shapes=[pltpu.VMEM((tm, c), F32)],
        compiler_params=_params(3),
        name="sdft",
    )(seq_table, z, zm)


def _conv_body(hp_ref, hm_ref, hn_ref, w8_ref, bdw_ref, g_ref, b_ref, o_ref, sh_ref, y_ref,
               *, ts, ns, width, lc, rc):
    i = pl.program_id(1)
    halo = CONV_HALO_ROWS
    c = hm_ref.shape[1]
    span = ts + 2 * halo - SUBLANES
    sh_ref[0, 0:halo, :] = jnp.where(i > 0, hp_ref[...], 0.0)
    sh_ref[0, halo:halo + ts, :] = hm_ref[...]
    sh_ref[0, halo + ts:halo + ts + halo, :] = jnp.where(i < ns - 1, hn_ref[...], 0.0)
    piece = 7 * SUBLANES
    for s in range(1, SUBLANES):
        for c0 in range(0, c, lc):
            for r0 in range(0, span, piece):
                n = min(piece, span - r0)
                sh_ref[s, r0:r0 + n, c0:c0 + lc] = sh_ref[0, r0 + s:r0 + s + n, c0:c0 + lc]

    off = halo - width // 2
    nq = rc // SUBLANES
    for c0 in range(0, c, lc):
        lanes = slice(c0, c0 + lc)

        def row_chunk(r0, lanes=lanes):
            acc = [None] * nq
            for t in range(width):
                a, s = divmod(t + off, SUBLANES)
                wv = w8_ref[t, :, lanes]
                for q in range(nq):
                    term = wv * sh_ref[s, pl.ds(r0 + SUBLANES * (q + a), SUBLANES), lanes]
                    acc[q] = term if acc[q] is None else acc[q] + term
            for q in range(nq):
                y_ref[pl.ds(r0 + SUBLANES * q, SUBLANES), lanes] = acc[q] + bdw_ref[:, lanes]

        _for_row_chunks(ts, rc, row_chunk)

    def finish(r0):
        rows = pl.ds(r0, SUBLANES)
        z = _layer_norm(y_ref[rows, :], g_ref[...], b_ref[...])
        o_ref[rows, :] = (z * jax.nn.sigmoid(z)).astype(o_ref.dtype)

    _for_row_chunks(ts, SUBLANES, finish, unroll=8)


def _conv(h, w_dw, b_dw, ln_g, ln_b, batch, s):
    m, c = h.shape
    width = w_dw.shape[0]
    halo = CONV_HALO_ROWS
    assert width // 2 <= halo and s % halo == 0
    ts = _tile(s, 256, 4 * halo)
    ns = s // ts
    rc = 4 * SUBLANES
    lc = _tile(c, 512, 128)
    hb, sb = ts // halo, s // halo
    last = m // halo - 1
    w8 = jnp.broadcast_to(w_dw[:, None, :], (width, SUBLANES, c))
    return pl.pallas_call(
        functools.partial(_conv_body, ts=ts, ns=ns, width=width, lc=lc, rc=rc),
        grid=(batch, ns),
        in_specs=[
            pl.BlockSpec((halo, c), lambda b, i: (jnp.maximum(b * sb + i * hb - 1, 0), 0)),
            pl.BlockSpec((ts, c), lambda b, i: (b * ns + i, 0)),
            pl.BlockSpec((halo, c), lambda b, i: (jnp.minimum(b * sb + (i + 1) * hb, last), 0)),
            pl.BlockSpec((width, SUBLANES, c), lambda b, i: (0, 0, 0)),
            pl.BlockSpec((1, c), lambda b, i: (0, 0)),
            pl.BlockSpec((1, c), lambda b, i: (0, 0)),
            pl.BlockSpec((1, c), lambda b, i: (0, 0)),
        ],
        out_specs=pl.BlockSpec((ts, c), lambda b, i: (b * ns + i, 0)),
        out_shape=jax.ShapeDtypeStruct((m, c), BF16),
        scratch_shapes=[pltpu.VMEM((SUBLANES, ts + 2 * halo, c), F32), pltpu.VMEM((ts, c), F32)],
        compiler_params=_params(2),
        name="conv",
    )(h, h, h, w8, b_dw, ln_g, ln_b)


def _mixout_body(y_ref, c_ref, wf_ref, wc_ref, ga_ref, gc_ref, wo_ref, r_ref, g_ref, b_ref,
                 o_ref, ob_ref, acc_ref, *, alpha, nj):
    j = pl.program_id(1)

    @pl.when(j == 0)
    def _():
        acc_ref[...] = jnp.zeros_like(acc_ref)

    @pl.when(j < nj)
    def _():
        a = _dot(y_ref[...], wf_ref[...])
        c = _dot(c_ref[...], wc_ref[...])
        mixed = (ga_ref[...] * a + gc_ref[...] * c).astype(BF16)
        acc_ref[...] += _dot(mixed, wo_ref[...])

    @pl.when(j >= nj)
    def _():
        _residual_ln_step(j - nj, acc_ref, r_ref, g_ref, b_ref, o_ref, ob_ref, alpha=alpha)


def _mixout(y, cin, w_fourier, w_conv_out, gates, w_out, x, ln_g, ln_b, *, alpha):
    m, kf = y.shape
    kc = cin.shape[1]
    d = w_out.shape[1]
    tm = _tile(m, 512, LN_STEP_ROWS)
    tn = _tile(d, 512, 128)
    nj = d // tn
    ne = tm // LN_STEP_ROWS
    jw = lambda j: jnp.minimum(j, nj - 1)
    ln_spec = _ln_step_specs(tm, d, nj)
    return pl.pallas_call(
        functools.partial(_mixout_body, alpha=alpha, nj=nj),
        grid=(m // tm, nj + ne),
        in_specs=[
            pl.BlockSpec((tm, kf), lambda i, j: (i, 0)),
            pl.BlockSpec((tm, kc), lambda i, j: (i, 0)),
            pl.BlockSpec((kf, tn), lambda i, j: (0, jw(j))),
            pl.BlockSpec((kc, tn), lambda i, j: (0, jw(j))),
            pl.BlockSpec((tm, tn), lambda i, j: (i, jw(j))),
            pl.BlockSpec((tm, tn), lambda i, j: (i, jw(j) + nj)),
            pl.BlockSpec((tn, d), lambda i, j: (jw(j), 0)),
            ln_spec,
            pl.BlockSpec((1, d), lambda i, j: (0, 0)),
            pl.BlockSpec((1, d), lambda i, j: (0, 0)),
        ],
        out_specs=[ln_spec, ln_spec],
        out_shape=[jax.ShapeDtypeStruct((m, d), F32), jax.ShapeDtypeStruct((m, d), BF16)],
        scratch_shapes=[pltpu.VMEM((tm, d), F32)],
        compiler_params=_params(2),
        name="mixout",
    )(y, cin, w_fourier, w_conv_out, gates, gates, w_out, x, ln_g, ln_b)


def _encoder_layer(x, xb, p, tables, batch, s, alpha):
    chan_table, seq_table = tables
    x, xb = _ffn(x, xb, p["w_ffn1_in"], p["w_ffn1_out"], p["ln_ffn1_g"], p["ln_ffn1_b"], alpha=alpha)
    u, h, gates = _proj(xb, p["w_in"], p["b_in"])
    z, zm = _cdft(u, chan_table, batch, s)
    y = _sdft(z, zm, seq_table)
    cin = _conv(h, p["w_dw"], p["b_dw"], p["ln_conv_g"], p["ln_conv_b"], batch, s)
    x, xb = _mixout(y, cin, p["w_fourier"], p["w_conv_out"], gates, p["w_out"], x,
                    p["ln_mix_g"], p["ln_mix_b"], alpha=alpha)
    x, xb = _ffn(x, xb, p["w_ffn2_in"], p["w_ffn2_out"], p["ln_ffn2_g"], p["ln_ffn2_b"], alpha=alpha)
    return x, xb


_MATMUL_WEIGHTS = ("w_ffn1_out", "w_in", "w_fourier", "w_conv_out", "w_out", "w_ffn2_out")
_GATE_UP_WEIGHTS = ("w_ffn1_in", "w_ffn2_in")
_HALF_STEP_WEIGHTS = ("w_ffn1_out", "w_ffn2_out")
_ROW_PARAMS = ("ln_ffn1_g", "ln_ffn1_b", "b_in", "b_dw", "ln_conv_g", "ln_conv_b",
               "ln_mix_g", "ln_mix_b", "ln_ffn2_g", "ln_ffn2_b")


def kernel(x_prompt, x_sample, ln_ffn1_g, ln_ffn1_b, w_ffn1_in, w_ffn1_out, w_in, b_in, w_fourier,
           w_dw, b_dw, ln_conv_g, ln_conv_b, w_conv_out, w_out, ln_mix_g, ln_mix_b,
           w_ffn2_in, w_ffn2_out, ln_ffn2_g, ln_ffn2_b):
    stacked = dict(ln_ffn1_g=ln_ffn1_g, ln_ffn1_b=ln_ffn1_b, w_ffn1_in=w_ffn1_in, w_ffn1_out=w_ffn1_out,
                   w_in=w_in, b_in=b_in, w_fourier=w_fourier, w_dw=w_dw, b_dw=b_dw,
                   ln_conv_g=ln_conv_g, ln_conv_b=ln_conv_b, w_conv_out=w_conv_out, w_out=w_out,
                   ln_mix_g=ln_mix_g, ln_mix_b=ln_mix_b, w_ffn2_in=w_ffn2_in, w_ffn2_out=w_ffn2_out,
                   ln_ffn2_g=ln_ffn2_g, ln_ffn2_b=ln_ffn2_b)
    depth = w_ffn1_in.shape[0]
    alpha = float((2 * depth) ** 0.25)
    layers = []
    for l in range(depth):
        p = {k: _cast_layer(stacked[k], l, 0.5 if k in _HALF_STEP_WEIGHTS else 1.0) for k in _MATMUL_WEIGHTS}
        p.update({k: _cast_gate_up(stacked[k], l) for k in _GATE_UP_WEIGHTS})
        p.update({k: stacked[k][l].reshape(1, -1) for k in _ROW_PARAMS})
        p["w_dw"] = w_dw[l]
        layers.append(p)

    cg = w_fourier.shape[1] // N_FOURIER_GROUPS

    def trunk(x):
        batch, s, d = x.shape
        tables = _dft_tables(s, cg)
        x = x.reshape(batch * s, d)
        xb = x.astype(BF16)
        for p in layers:
            x, xb = _encoder_layer(x, xb, p, tables, batch, s, alpha)
        return x.reshape(batch, s, d)

    return trunk(x_prompt), trunk(x_sample)
```
